```python
import math
import jax
import jax.numpy as jnp
from jax import lax
import numpy as np

D_MODEL = 2048
BATCH = 2
SEQ = 16384
DEPTH = 4

GRID_W = 64
CTX_LEN = 256
N_MIXERS = 3
N_FOURIER_LAYERS = (DEPTH + 2) // 3
N_GLOBAL_LAYERS = (DEPTH + 1) // 3
N_WINDOW_LAYERS = DEPTH // 3
HEAD_DIM = 128
N_HEADS = D_MODEL // HEAD_DIM
N_KV_HEADS = 4
GQA_GROUP = N_HEADS // N_KV_HEADS
Q_DIM = N_HEADS * HEAD_DIM
KV_DIM = N_KV_HEADS * HEAD_DIM
QKV_DIM = Q_DIM + 2 * KV_DIM
ATTN_SCALE = HEAD_DIM ** -0.5
ROPE_THETA = 10000.0
ROT_AXIS_DIM = HEAD_DIM // 2
ROT_FREQS = ROT_AXIS_DIM // 2
Q_BLOCK = 128
WINDOW = 128
FOURIER_GROUPS = 8
N_GROUPS = 4
EXPERTS_PER_GROUP = 8
N_EXPERTS = N_GROUPS * EXPERTS_PER_GROUP
TOP_K = 2
EXPERT_FF = 512
N_MOD = 6
EPS = 1e-6

kernel_name = "hybrid_fnet_gqa_swa_hmoe_dit"


def _rmsnorm(x, g):
    xf = x.astype(jnp.float32)
    y = xf * lax.rsqrt(jnp.mean(xf * xf, axis=-1, keepdims=True) + EPS)
    return (y * g.astype(jnp.float32)).astype(x.dtype)


def _modulate(h, shift, scale):
    return h * (1 + scale) + shift


def _axial_rope_tables(S):
    rows = S // GRID_W
    row = jnp.broadcast_to(jnp.arange(rows)[:, None], (rows, GRID_W)).reshape(-1).astype(jnp.float32)
    col = jnp.broadcast_to(jnp.arange(GRID_W)[None, :], (rows, GRID_W)).reshape(-1).astype(jnp.float32)
    inv_freq = ROPE_THETA ** (-jnp.arange(ROT_FREQS, dtype=jnp.float32) / ROT_FREQS)
    ang = jnp.stack([row[:, None] * inv_freq, col[:, None] * inv_freq], axis=1)
    return jnp.cos(ang), jnp.sin(ang)


def _rope(x, cos, sin):
    L = x.shape[1]
    shape = (L,) + (1,) * (x.ndim - 3) + (2, ROT_FREQS)
    cos = cos.reshape(shape).astype(x.dtype)
    sin = sin.reshape(shape).astype(x.dtype)
    xr = x.reshape(x.shape[:-1] + (2, 2, ROT_FREQS))
    xa, xb = xr[..., 0, :], xr[..., 1, :]
    out = jnp.stack([xa * cos - xb * sin, xb * cos + xa * sin], axis=-2)
    return out.reshape(x.shape)


def _qkv(h, w_qkv):
    B, L, _ = h.shape
    qkv = h @ w_qkv
    q = qkv[..., :Q_DIM].reshape(B, L, N_KV_HEADS, GQA_GROUP, HEAD_DIM)
    k = qkv[..., Q_DIM:Q_DIM + KV_DIM].reshape(B, L, N_KV_HEADS, HEAD_DIM)
    v = qkv[..., Q_DIM + KV_DIM:].reshape(B, L, N_KV_HEADS, HEAD_DIM)
    return q, k, v


def _attend(q, k, v):
    s = jnp.einsum("bqhgd,bkhd->bhgqk", q, k).astype(jnp.float32) * ATTN_SCALE
    p = jax.nn.softmax(s, axis=-1)
    return jnp.einsum("bhgqk,bkhd->bqhgd", p.astype(v.dtype), v)


def _sink_attend(q, segments, sink):
    logits = []
    for k, _, mask in segments:
        s = jnp.einsum("bqhgd,bkhd->bhgqk", q, k).astype(jnp.float32) * ATTN_SCALE
        if mask is not None:
            s = jnp.where(mask, s, -jnp.inf)
        logits.append(s)
    sink_col = jnp.broadcast_to(sink.astype(jnp.float32).reshape(N_KV_HEADS, GQA_GROUP, 1, 1),
                                logits[0].shape[:-1] + (1,))
    p = jax.nn.softmax(jnp.concatenate(logits + [sink_col], axis=-1), axis=-1)
    outs = []
    off = 0
    for k, v, _ in segments:
        n = k.shape[1]
        outs.append(jnp.einsum("bhgqk,bkhd->bqhgd", p[..., off:off + n].astype(v.dtype), v))
        off += n
    return sum(outs[1:], outs[0])


def _fourier_mix(h, w_out):
    B, L, D = h.shape
    hg = h.astype(jnp.float32).reshape(B, L, FOURIER_GROUPS, D // FOURIER_GROUPS)
    mixed = jnp.fft.fft2(hg, axes=(1, 3), norm="ortho").real
    return mixed.reshape(B, L, D).astype(h.dtype) @ w_out


def _global_attn(h_lat, h_ctx, w_qkv, q_gain, k_gain, w_o, cos, sin, ctx_out):
    B, S, _ = h_lat.shape
    q, k, v = _qkv(h_lat, w_qkv)
    q = _rope(_rmsnorm(q, q_gain), cos, sin)
    k = _rope(_rmsnorm(k, k_gain), cos, sin)
    qc, kc, vc = _qkv(h_ctx, w_qkv)
    kc = _rmsnorm(kc, k_gain)
    k_all = jnp.concatenate([kc, k], axis=1)
    v_all = jnp.concatenate([vc, v], axis=1)
    nb = S // Q_BLOCK
    q_blocks = jnp.moveaxis(q.reshape(B, nb, Q_BLOCK, N_KV_HEADS, GQA_GROUP, HEAD_DIM), 1, 0)
    o = lax.map(lambda qb: _attend(qb, k_all, v_all), q_blocks)
    y_lat = jnp.moveaxis(o, 0, 1).reshape(B, S, Q_DIM) @ w_o
    y_ctx = None
    if ctx_out:
        oc = _attend(_rmsnorm(qc, q_gain), kc, vc)
        y_ctx = oc.reshape(B, h_ctx.shape[1], Q_DIM) @ w_o
    return y_lat, y_ctx


def _window_attn(h_lat, h_ctx, w_qkv, sink, w_o, cos, sin, ctx_out):
    B, S, _ = h_lat.shape
    q, k, v = _qkv(h_lat, w_qkv)
    q = _rope(q, cos, sin)
    k = _rope(k, cos, sin)
    qc, kc, vc = _qkv(h_ctx, w_qkv)
    nb = S // Q_BLOCK
    span = Q_BLOCK + 2 * WINDOW
    kpad = jnp.pad(k, ((0, 0), (WINDOW, WINDOW), (0, 0), (0, 0)))
    vpad = jnp.pad(v, ((0, 0), (WINDOW, WINDOW), (0, 0), (0, 0)))
    q_blocks = jnp.moveaxis(q.reshape(B, nb, Q_BLOCK, N_KV_HEADS, GQA_GROUP, HEAD_DIM), 1, 0)

    def block(args):
        n, qb = args
        start = n * Q_BLOCK
        kw = lax.dynamic_slice_in_dim(kpad, start, span, axis=1)
        vw = lax.dynamic_slice_in_dim(vpad, start, span, axis=1)
        qpos = start + jnp.arange(Q_BLOCK)
        kpos = start - WINDOW + jnp.arange(span)
        valid = (jnp.abs(qpos[:, None] - kpos[None, :]) <= WINDOW) & (kpos[None, :] >= 0) & (kpos[None, :] < S)
        return _sink_attend(qb, [(kc, vc, None), (kw, vw, valid)], sink)

    o = lax.map(block, (jnp.arange(nb), q_blocks))
    y_lat = jnp.moveaxis(o, 0, 1).reshape(B, S, Q_DIM) @ w_o
    y_ctx = None
    if ctx_out:
        oc = _sink_attend(qc, [(kc, vc, None)], sink)
        y_ctx = oc.reshape(B, h_ctx.shape[1], Q_DIM) @ w_o
    return y_lat, y_ctx


def _hier_moe(h, rg_w, rg_b, re_w, re_b, w_gate, w_up, w_down):
    hf = h.astype(jnp.float32)
    g_prob = jax.nn.softmax(hf @ rg_w.astype(jnp.float32) + rg_b.astype(jnp.float32), axis=-1)
    g_top, g_idx = lax.top_k(g_prob, 1)
    e_logits = (hf @ re_w.astype(jnp.float32) + re_b.astype(jnp.float32)).reshape(-1, N_GROUPS, EXPERTS_PER_GROUP)
    e_logits = jnp.einsum("nge,ng->ne", e_logits, jax.nn.one_hot(g_idx[:, 0], N_GROUPS, dtype=jnp.float32))
    e_top, e_idx = lax.top_k(jax.nn.softmax(e_logits, axis=-1), TOP_K)
    e_w = e_top / jnp.sum(e_top, axis=-1, keepdims=True)
    gid = g_idx * EXPERTS_PER_GROUP + e_idx
    combine = jnp.sum(jax.nn.one_hot(gid, N_EXPERTS, dtype=jnp.float32) * e_w[..., None], axis=1).astype(h.dtype)
    y = jnp.zeros_like(h)
    for e in range(N_EXPERTS):
        a = h @ w_gate[e]
        u = h @ w_up[e]
        y = y + combine[:, e:e + 1] * ((jax.nn.silu(a) * u) @ w_down[e])
    return y, g_top.astype(h.dtype)


def setup_inputs(seed: int = 0) -> dict:
    key = jax.random.key(seed)
    ks = jax.random.split(key, 26)
    f32 = jnp.float32
    D = D_MODEL

    def nrm(k, shape, scale):
        return jax.random.normal(k, shape, f32) * scale

    def gain(k, shape):
        return 1.0 + 0.02 * jax.random.normal(k, shape, f32)

    return {
        "x": nrm(ks[0], (BATCH, SEQ, D), 1.0),
        "c": nrm(ks[1], (BATCH, D), 1.0),
        "ctx": nrm(ks[2], (BATCH, CTX_LEN, D), 1.0),
        "c_ctx": nrm(ks[3], (D,), 1.0),
        "pre_norm_mix": gain(ks[4], (DEPTH, D)),
        "post_norm_mix": gain(ks[5], (DEPTH, D)),
        "pre_norm_ffn": gain(ks[6], (DEPTH, D)),
        "post_norm_ffn": gain(ks[7], (DEPTH, D)),
        "w_mod": nrm(ks[8], (DEPTH, D, N_MOD * D), 0.5 * D ** -0.5),
        "b_mod": nrm(ks[9], (DEPTH, N_MOD * D), 0.02),
        "fnet_w_out": nrm(ks[10], (N_FOURIER_LAYERS, D, D), D ** -0.5),
        "gqa_w_qkv": nrm(ks[11], (N_GLOBAL_LAYERS, D, QKV_DIM), D ** -0.5),
        "gqa_q_norm": gain(ks[12], (N_GLOBAL_LAYERS, HEAD_DIM)),
        "gqa_k_norm": gain(ks[13], (N_GLOBAL_LAYERS, HEAD_DIM)),
        "gqa_w_o": nrm(ks[14], (N_GLOBAL_LAYERS, Q_DIM, D), Q_DIM ** -0.5),
        "swa_w_qkv": nrm(ks[15], (N_WINDOW_LAYERS, D, QKV_DIM), D ** -0.5),
        "swa_sink": nrm(ks[16], (N_WINDOW_LAYERS, N_HEADS), 0.5),
        "swa_w_o": nrm(ks[17], (N_WINDOW_LAYERS, Q_DIM, D), Q_DIM ** -0.5),
        "router_group_w": nrm(ks[18], (DEPTH, D, N_GROUPS), D ** -0.5),
        "router_group_b": nrm(ks[19], (DEPTH, N_GROUPS), 0.01),
        "router_expert_w": nrm(ks[20], (DEPTH, D, N_EXPERTS), D ** -0.5),
        "router_expert_b": nrm(ks[21], (DEPTH, N_EXPERTS), 0.01),
        "expert_w_gate": nrm(ks[22], (DEPTH, N_EXPERTS, D, EXPERT_FF), D ** -0.5),
        "expert_w_up": nrm(ks[23], (DEPTH, N_EXPERTS, D, EXPERT_FF), D ** -0.5),
        "expert_w_down": nrm(ks[24], (DEPTH, N_EXPERTS, EXPERT_FF, D), EXPERT_FF ** -0.5),
    }


def reference(x, c, ctx, c_ctx, pre_norm_mix, post_norm_mix, pre_norm_ffn, post_norm_ffn,
              w_mod, b_mod, fnet_w_out, gqa_w_qkv, gqa_q_norm, gqa_k_norm, gqa_w_o,
              swa_w_qkv, swa_sink, swa_w_o, router_group_w, router_group_b,
              router_expert_w, router_expert_b, expert_w_gate, expert_w_up, expert_w_down):
    B, S, D = x.shape
    C = ctx.shape[1]
    cos, sin = _axial_rope_tables(S)
    ctx_h = ctx
    for i in range(DEPTH):
        kind = i % N_MIXERS
        j = i // N_MIXERS
        ctx_out = i < DEPTH - 1
        m_lat = (jax.nn.silu(c) @ w_mod[i] + b_mod[i]).reshape(B, N_MOD, 1, D)
        m_ctx = (jax.nn.silu(c_ctx) @ w_mod[i] + b_mod[i]).reshape(N_MOD, D)

        h_lat = _modulate(_rmsnorm(x, pre_norm_mix[i]), m_lat[:, 0], m_lat[:, 1])
        need_ctx_in = ctx_out or kind != 0
        h_ctx = _modulate(_rmsnorm(ctx_h, pre_norm_mix[i]), m_ctx[0], m_ctx[1]) if need_ctx_in else None
        if kind == 0:
            y_lat = _fourier_mix(h_lat, fnet_w_out[j])
            y_ctx = _fourier_mix(h_ctx, fnet_w_out[j]) if ctx_out else None
        elif kind == 1:
            y_lat, y_ctx = _global_attn(h_lat, h_ctx, gqa_w_qkv[j], gqa_q_norm[j], gqa_k_norm[j],
                                        gqa_w_o[j], cos, sin, ctx_out)
        else:
            y_lat, y_ctx = _window_attn(h_lat, h_ctx, swa_w_qkv[j], swa_sink[j], swa_w_o[j],
                                        cos, sin, ctx_out)
        x = x + m_lat[:, 2] * _rmsnorm(y_lat, post_norm_mix[i])
        if ctx_out:
            ctx_h = ctx_h + m_ctx[2] * _rmsnorm(y_ctx, post_norm_mix[i])

        tok = _modulate(_rmsnorm(x, pre_norm_ffn[i]), m_lat[:, 3], m_lat[:, 4]).reshape(B * S, D)
        if ctx_out:
            tok_c = _modulate(_rmsnorm(ctx_h, pre_norm_ffn[i]), m_ctx[3], m_ctx[4]).reshape(B * C, D)
            tok = jnp.concatenate([tok, tok_c], axis=0)
        y, p_group = _hier_moe(tok, router_group_w[i], router_group_b[i], router_expert_w[i],
                               router_expert_b[i], expert_w_gate[i], expert_w_up[i], expert_w_down[i])
        y = p_group * _rmsnorm(y, post_norm_ffn[i])
        x = x + m_lat[:, 5] * y[:B * S].reshape(B, S, D)
        if ctx_out:
            ctx_h = ctx_h + m_ctx[5] * y[B * S:].reshape(B, C, D)
    return x
```

```python
import functools
import math

import numpy as np
import jax
import jax.numpy as jnp
from jax import lax
from jax.experimental import pallas as pl
from jax.experimental.pallas import tpu as pltpu

F32 = jnp.float32
BF16 = jnp.bfloat16

D = 2048
HD = 128
NH = 16
NKV = 4
GQ = NH // NKV
QKV = (NH + 2 * NKV) * HD
GRID_W = 64
ROPE_THETA = 10000.0
ROT_FREQS = HD // 4
WINDOW = 128
FG = 8
FC = D // FG
NG = 4
EPG = 8
NE = NG * EPG
FF = 512
NMOD = 6
EPS = 1e-6
ATTN_SCALE = HD ** -0.5
DEPTH = 4

LANES = 128
SUB = 8
VMEM_LIMIT = 56 * 1024 * 1024

TM_QKV = 512
TN_QKV = 256
TM_OPROJ = 256
TQ_FLASH = 1024
TK_FLASH = 1024
TQ_WIN = 256
TM_MOE = 256
TMX = 256
TD_F = 512
N2 = 128
NEG = -1e30


def _cp(sem, vmem=VMEM_LIMIT):
    return pltpu.CompilerParams(dimension_semantics=sem, vmem_limit_bytes=vmem)


def _dot(a, b):
    return jnp.dot(a, b, preferred_element_type=F32)


def _dot_nt(a, b):
    return lax.dot_general(a, b, (((1,), (1,)), ((), ())), preferred_element_type=F32)


def _rms(x):
    return x * lax.rsqrt(jnp.mean(x * x, axis=-1, keepdims=True) + EPS)


def _mod_kernel(c_ref, w_ref, b_ref, o_ref):
    a = c_ref[...]
    a = a * jax.nn.sigmoid(a)
    o_ref[...] = _dot(a.astype(BF16), w_ref[...].astype(BF16)) + b_ref[...]


def _modulation(cvec, w_mod, b_mod):
    depth, _, nm = w_mod.shape
    tn = 1024
    return pl.pallas_call(
        _mod_kernel,
        grid=(depth, nm // tn),
        in_specs=[
            pl.BlockSpec((SUB, D), lambda l, j: (0, 0)),
            pl.BlockSpec((None, D, tn), lambda l, j: (l, 0, j)),
            pl.BlockSpec((None, 1, tn), lambda l, j: (l, 0, j)),
        ],
        out_specs=pl.BlockSpec((None, SUB, tn), lambda l, j: (l, 0, j)),
        out_shape=jax.ShapeDtypeStruct((depth, SUB, nm), F32),
        compiler_params=_cp(("arbitrary", "arbitrary")),
        name="modulation",
    )(cvec, w_mod, b_mod.reshape(depth, 1, nm))


def _qkv_kernel(x_ref, g_ref, mod_ref, w_ref, hg_ref, cos_ref, sin_ref, o_ref, h_scr, *, do_norm, n_rot):
    j = pl.program_id(1)

    @pl.when(j == 0)
    def _():
        h = _rms(x_ref[...]) * g_ref[...]
        h = h * (1.0 + mod_ref[1:2, :]) + mod_ref[0:1, :]
        h_scr[...] = h.astype(BF16)

    y = _dot(h_scr[...], w_ref[...])

    @pl.when(j < n_rot)
    def _():
        outs = []
        for hh in range(TN_QKV // HD):
            yh = y[:, hh * HD:(hh + 1) * HD]
            if do_norm:
                yh = _rms(yh)
            yh = yh * hg_ref[:, hh * HD:(hh + 1) * HD]
            lane = lax.broadcasted_iota(jnp.int32, yh.shape, 1)
            first = (lane % (2 * ROT_FREQS)) < ROT_FREQS
            partner = jnp.where(first, pltpu.roll(yh, HD - ROT_FREQS, 1), pltpu.roll(yh, ROT_FREQS, 1))
            outs.append(yh * cos_ref[...] + partner * sin_ref[...])
        o_ref[...] = jnp.concatenate(outs, axis=1).astype(BF16)

    @pl.when(j >= n_rot)
    def _():
        o_ref[...] = y.astype(BF16)


def _qkv_proj(x_all, gain, mod, w_bf, head_gain, cos_e, sin_e, tpb, nb, do_norm):
    r = x_all.shape[0]
    nj = QKV // TN_QKV
    n_rot = (NH + NKV) * HD // TN_QKV
    return pl.pallas_call(
        functools.partial(_qkv_kernel, do_norm=do_norm, n_rot=n_rot),
        grid=(r // TM_QKV, nj),
        in_specs=[
            pl.BlockSpec((TM_QKV, D), lambda i, j: (i, 0)),
            pl.BlockSpec((1, D), lambda i, j: (0, 0)),
            pl.BlockSpec((None, NMOD, D), lambda i, j: (jnp.minimum(i // tpb, nb), 0, 0)),
            pl.BlockSpec((D, TN_QKV), lambda i, j: (0, j)),
            pl.BlockSpec((None, 1, TN_QKV), lambda i, j: (j, 0, 0)),
            pl.BlockSpec((TM_QKV, HD), lambda i, j: (i, 0)),
            pl.BlockSpec((TM_QKV, HD), lambda i, j: (i, 0)),
        ],
        out_specs=pl.BlockSpec((TM_QKV, TN_QKV), lambda i, j: (i, j)),
        out_shape=jax.ShapeDtypeStruct((r, QKV), BF16),
        scratch_shapes=[pltpu.VMEM((TM_QKV, D), BF16)],
        compiler_params=_cp(("parallel", "arbitrary")),
        name="qkv_proj",
    )(x_all, gain, mod, w_bf, head_gain, cos_e, sin_e)


def _flash_kernel(q_ref, kc_ref, vc_ref, k_ref, v_ref, o_ref, m_scr, l_scr, acc_scr):
    j = pl.program_id(3)

    def update(k, v):
        for g in range(GQ):
            q = q_ref[:, g * HD:(g + 1) * HD]
            s = _dot_nt(q, k)
            m_prev = m_scr[g]
            m_new = jnp.maximum(m_prev, jnp.max(s, axis=-1, keepdims=True))
            alpha = jnp.exp(m_prev - m_new)
            p = jnp.exp(s - m_new)
            l_scr[g] = alpha * l_scr[g] + jnp.sum(p, axis=-1, keepdims=True)
            acc_scr[g] = alpha * acc_scr[g] + _dot(p.astype(BF16), v)
            m_scr[g] = m_new

    @pl.when(j == 0)
    def _():
        m_scr[...] = jnp.full(m_scr.shape, NEG, F32)
        l_scr[...] = jnp.zeros(l_scr.shape, F32)
        acc_scr[...] = jnp.zeros(acc_scr.shape, F32)
        update(kc_ref[...], vc_ref[...])

    update(k_ref[...], v_ref[...])

    @pl.when(j == pl.num_programs(3) - 1)
    def _():
        for g in range(GQ):
            o_ref[:, g * HD:(g + 1) * HD] = (acc_scr[g] / l_scr[g]).astype(BF16)


def _global_attention(qkv, nb, s, c):
    nq = s // TQ_FLASH
    nk = s // TK_FLASH
    kcol = NH
    vcol = NH + NKV
    cblk = nb * s // c
    return pl.pallas_call(
        _flash_kernel,
        grid=(nb, NKV, nq, nk),
        in_specs=[
            pl.BlockSpec((TQ_FLASH, GQ * HD), lambda b, h, i, j: (b * nq + i, h)),
            pl.BlockSpec((c, HD), lambda b, h, i, j: (cblk + b, kcol + h)),
            pl.BlockSpec((c, HD), lambda b, h, i, j: (cblk + b, vcol + h)),
            pl.BlockSpec((TK_FLASH, HD), lambda b, h, i, j: (b * nk + j, kcol + h)),
            pl.BlockSpec((TK_FLASH, HD), lambda b, h, i, j: (b * nk + j, vcol + h)),
        ],
        out_specs=pl.BlockSpec((TQ_FLASH, GQ * HD), lambda b, h, i, j: (b * nq + i, h)),
        out_shape=jax.ShapeDtypeStruct((nb * s, NH * HD), BF16),
        scratch_shapes=[
            pltpu.VMEM((GQ, TQ_FLASH, 1), F32),
            pltpu.VMEM((GQ, TQ_FLASH, 1), F32),
            pltpu.VMEM((GQ, TQ_FLASH, HD), F32),
        ],
        compiler_params=_cp(("parallel", "parallel", "parallel", "arbitrary")),
        name="global_attention",
    )(qkv, qkv, qkv, qkv, qkv)


def _window_kernel(sink_ref, q_ref, kc_ref, vc_ref, kp_ref, vp_ref, k0_ref, v0_ref, kn_ref, vn_ref, o_ref, *, s_len):
    h = pl.program_id(1)
    i = pl.program_id(2)
    tq = q_ref.shape[0]
    qpos = i * tq + lax.broadcasted_iota(jnp.int32, (tq, tq), 0)
    col = lax.broadcasted_iota(jnp.int32, (tq, tq), 1)
    segs = [(kc_ref, vc_ref, None)]
    for off, kr, vr in ((-1, kp_ref, vp_ref), (0, k0_ref, v0_ref), (1, kn_ref, vn_ref)):
        kpos = (i + off) * tq + col
        valid = (jnp.abs(qpos - kpos) <= WINDOW) & (kpos >= 0) & (kpos < s_len)
        segs.append((kr, vr, valid))
    for g in range(GQ):
        q = q_ref[:, g * HD:(g + 1) * HD]
        sink = sink_ref[h * GQ + g]
        scores = []
        m = jnp.full((tq, 1), sink, F32)
        for kr, _, valid in segs:
            sc = _dot_nt(q, kr[...])
            if valid is not None:
                sc = jnp.where(valid, sc, NEG)
            scores.append(sc)
            m = jnp.maximum(m, jnp.max(sc, axis=-1, keepdims=True))
        l = jnp.exp(sink - m)
        acc = jnp.zeros((tq, HD), F32)
        for sc, (_, vr, valid) in zip(scores, segs):
            p = jnp.exp(sc - m)
            if valid is not None:
                p = jnp.where(valid, p, 0.0)
            l = l + jnp.sum(p, axis=-1, keepdims=True)
            acc = acc + _dot(p.astype(BF16), vr[...])
        o_ref[:, g * HD:(g + 1) * HD] = (acc / l).astype(BF16)


def _window_attention(qkv, sink, nb, s, c):
    tq = TQ_WIN
    nq = s // tq
    kcol = NH
    vcol = NH + NKV
    cblk = nb * s // c

    def kv(col, off):
        return pl.BlockSpec((tq, HD), lambda b, h, i: (b * nq + jnp.clip(i + off, 0, nq - 1), col + h))

    return pl.pallas_call(
        functools.partial(_window_kernel, s_len=s),
        grid=(nb, NKV, nq),
        in_specs=[
            pl.BlockSpec(memory_space=pltpu.SMEM),
            pl.BlockSpec((tq, GQ * HD), lambda b, h, i: (b * nq + i, h)),
            pl.BlockSpec((c, HD), lambda b, h, i: (cblk + b, kcol + h)),
            pl.BlockSpec((c, HD), lambda b, h, i: (cblk + b, vcol + h)),
            kv(kcol, -1), kv(vcol, -1), kv(kcol, 0), kv(vcol, 0), kv(kcol, 1), kv(vcol, 1),
        ],
        out_specs=pl.BlockSpec((tq, GQ * HD), lambda b, h, i: (b * nq + i, h)),
        out_shape=jax.ShapeDtypeStruct((nb * s, NH * HD), BF16),
        compiler_params=_cp(("parallel", "parallel", "parallel")),
        name="window_attention",
    )(sink, qkv, qkv, qkv, qkv, qkv, qkv, qkv, qkv, qkv)


def _ctx_attn_kernel(sink_ref, q_ref, k_ref, v_ref, o_ref, *, use_sink):
    h = pl.program_id(1)
    k = k_ref[...]
    v = v_ref[...]
    for g in range(GQ):
        s = _dot_nt(q_ref[:, g * HD:(g + 1) * HD], k)
        m = jnp.max(s, axis=-1, keepdims=True)
        if use_sink:
            sink = sink_ref[h * GQ + g]
            m = jnp.maximum(m, sink)
        p = jnp.exp(s - m)
        l = jnp.sum(p, axis=-1, keepdims=True)
        if use_sink:
            l = l + jnp.exp(sink - m)
        o_ref[:, g * HD:(g + 1) * HD] = (_dot(p.astype(BF16), v) / l).astype(BF16)


def _ctx_attention(qkv, sink, nb, s, c, use_sink):
    cblk = nb * s // c
    return pl.pallas_call(
        functools.partial(_ctx_attn_kernel, use_sink=use_sink),
        grid=(nb, NKV),
        in_specs=[
            pl.BlockSpec(memory_space=pltpu.SMEM),
            pl.BlockSpec((c, GQ * HD), lambda b, h: (cblk + b, h)),
            pl.BlockSpec((c, HD), lambda b, h: (cblk + b, NH + h)),
            pl.BlockSpec((c, HD), lambda b, h: (cblk + b, NH + NKV + h)),
        ],
        out_specs=pl.BlockSpec((c, GQ * HD), lambda b, h: (b, h)),
        out_shape=jax.ShapeDtypeStruct((nb * c, NH * HD), BF16),
        compiler_params=_cp(("arbitrary", "arbitrary")),
        name="ctx_attention",
    )(sink, qkv, qkv, qkv)


def _oproj_kernel(a_ref, ac_ref, w_ref, x_ref, g_ref, mod_ref, o_ref, *, nlat):
    def project(src_ref):
        y = _dot(src_ref[...].astype(BF16), w_ref[...])
        o_ref[...] = x_ref[...] + mod_ref[2:3, :] * (_rms(y) * g_ref[...])

    i = pl.program_id(0)

    @pl.when(i < nlat)
    def _():
        project(a_ref)

    @pl.when(i >= nlat)
    def _():
        project(ac_ref)


def _oproj(a_lat, a_ctx, w_bf, x_all, gain, mod, tpb, nb, with_ctx):
    tm = TM_OPROJ
    nlat = a_lat.shape[0] // tm
    ntiles = nlat + (a_ctx.shape[0] // tm if with_ctx else 0)
    return pl.pallas_call(
        functools.partial(_oproj_kernel, nlat=nlat),
        grid=(ntiles,),
        in_specs=[
            pl.BlockSpec((tm, D), lambda i: (jnp.minimum(i, nlat - 1), 0)),
            pl.BlockSpec((tm, D), lambda i: (jnp.maximum(i - nlat, 0), 0)),
            pl.BlockSpec((D, D), lambda i: (0, 0)),
            pl.BlockSpec((tm, D), lambda i: (i, 0)),
            pl.BlockSpec((1, D), lambda i: (0, 0)),
            pl.BlockSpec((None, NMOD, D), lambda i: (jnp.minimum(i // tpb, nb), 0, 0)),
        ],
        out_specs=pl.BlockSpec((tm, D), lambda i: (i, 0)),
        out_shape=jax.ShapeDtypeStruct(x_all.shape, F32),
        input_output_aliases={3: 0},
        compiler_params=_cp(("parallel",)),
        name="oproj",
    )(a_lat, a_ctx, w_bf, x_all, gain, mod)


def _dft_mats(n):
    k = np.arange(n)
    ang = 2.0 * np.pi * ((k[:, None] * k[None, :]) % n) / n
    return np.cos(ang) / math.sqrt(n), np.sin(ang) / math.sqrt(n)


def _stage1_matrix(n1):
    c, s = _dft_mats(n1)
    eye = np.eye(SUB)
    return np.concatenate([np.kron(c, eye), -np.kron(s, eye)], axis=0)


def _stage2_matrix():
    c, s = _dft_mats(N2)
    eye = np.eye(SUB)
    pc = np.einsum("kn,ab->kabn", c, eye).reshape(N2 * SUB, SUB * N2)
    ps = np.einsum("kn,ab->kabn", s, eye).reshape(N2 * SUB, SUB * N2)
    return np.block([[pc, ps], [-ps, pc]])


def _channel_matrix():
    c, s = _dft_mats(FC)
    return np.concatenate([c, s], axis=0)


def _fft1_kernel(x_ref, g_ref, mod_ref, kr_ref, tc_ref, ts_ref, ar_ref, ai_ref, rstd_scr):
    dt = pl.program_id(2)
    n1 = x_ref.shape[0]
    rows = n1 * SUB

    @pl.when(dt == 0)
    def _():
        x = x_ref[...].reshape(rows, D)
        rstd_scr[...] = lax.rsqrt(jnp.mean(x * x, axis=-1, keepdims=True) + EPS)

    col = pl.multiple_of(dt * TD_F, TD_F)
    x = x_ref[:, :, pl.ds(col, TD_F)].reshape(rows, TD_F)
    h = x * rstd_scr[...] * g_ref[...]
    h = h * (1.0 + mod_ref[1:2, :]) + mod_ref[0:1, :]
    y = _dot(kr_ref[...], h.astype(BF16))
    yr = y[:rows]
    yi = y[rows:]
    tc = tc_ref[...]
    ts = ts_ref[...]
    for cc in range(TD_F // LANES):
        sl = slice(cc * LANES, (cc + 1) * LANES)
        a = yr[:, sl]
        b = yi[:, sl]
        ar_ref[:, :, sl] = (a * tc + b * ts).reshape(n1, SUB, LANES)
        ai_ref[:, :, sl] = (b * tc - a * ts).reshape(n1, SUB, LANES)


def _fft2_kernel(ar_ref, ai_ref, k2_ref, cs_ref, o_ref):
    a = jnp.concatenate([ar_ref[...].astype(BF16), ai_ref[...].astype(BF16)], axis=0)
    y = _dot(k2_ref[...], a)
    rows = N2 * SUB
    yr = y[:rows].astype(BF16)
    yi = y[rows:].astype(BF16)
    for gi in range(TD_F // FC):
        sl = slice(gi * FC, (gi + 1) * FC)
        z = jnp.concatenate([yr[:, sl], yi[:, sl]], axis=1)
        o_ref[:, :, sl] = _dot(z, cs_ref[...]).reshape(N2, SUB, FC)


def _fnet_ctx_kernel(x_ref, g_ref, mod_ref, f_ref, cs_ref, o_ref):
    c = x_ref.shape[0]
    h = _rms(x_ref[...]) * g_ref[...]
    h = h * (1.0 + mod_ref[1:2, :]) + mod_ref[0:1, :]
    y = _dot(f_ref[...], h.astype(BF16))
    yr = y[:c].astype(BF16)
    yi = y[c:].astype(BF16)
    for gi in range(FG):
        sl = slice(gi * FC, (gi + 1) * FC)
        z = jnp.concatenate([yr[:, sl], yi[:, sl]], axis=1)
        o_ref[:, sl] = _dot(z, cs_ref[...])


def _fourier_mix(x_all, gain, mod, nb, s, c, with_ctx):
    r = x_all.shape[0]
    n1 = s // N2
    nch = N2 // SUB
    ndt = D // TD_F
    rows = n1 * SUB
    kr = jnp.asarray(_stage1_matrix(n1), BF16)
    k2 = jnp.asarray(_stage2_matrix(), BF16)
    cs = jnp.asarray(_channel_matrix(), BF16)
    k1 = jnp.arange(n1, dtype=jnp.int32)[None, :, None]
    n2 = (jnp.arange(nch, dtype=jnp.int32)[:, None, None] * SUB + jnp.arange(SUB, dtype=jnp.int32)[None, None, :])
    ang = (2.0 * np.pi / s) * ((k1 * n2) % s).astype(F32).reshape(nch, rows, 1)
    tc = jnp.broadcast_to(jnp.cos(ang), (nch, rows, LANES))
    ts = jnp.broadcast_to(jnp.sin(ang), (nch, rows, LANES))

    x4 = x_all.reshape(r // N2, N2, D)
    ar, ai = pl.pallas_call(
        _fft1_kernel,
        grid=(nb, nch, ndt),
        in_specs=[
            pl.BlockSpec((n1, SUB, D), lambda b, ch, dt: (b, ch, 0)),
            pl.BlockSpec((1, TD_F), lambda b, ch, dt: (0, dt)),
            pl.BlockSpec((None, NMOD, TD_F), lambda b, ch, dt: (b, 0, dt)),
            pl.BlockSpec((2 * rows, rows), lambda b, ch, dt: (0, 0)),
            pl.BlockSpec((None, rows, LANES), lambda b, ch, dt: (ch, 0, 0)),
            pl.BlockSpec((None, rows, LANES), lambda b, ch, dt: (ch, 0, 0)),
        ],
        out_specs=[
            pl.BlockSpec((n1, SUB, TD_F), lambda b, ch, dt: (b, ch, dt)),
            pl.BlockSpec((n1, SUB, TD_F), lambda b, ch, dt: (b, ch, dt)),
        ],
        out_shape=[jax.ShapeDtypeStruct((nb * n1, N2, D), F32)] * 2,
        scratch_shapes=[pltpu.VMEM((rows, 1), F32)],
        compiler_params=_cp(("parallel", "parallel", "arbitrary")),
        name="fnet_stage1",
    )(x4, gain, mod, kr, tc, ts)

    ar2 = ar.reshape(nb * s, D)
    ai2 = ai.reshape(nb * s, D)
    nkc = n1 // SUB
    mixed = pl.pallas_call(
        _fft2_kernel,
        grid=(nb, nkc, ndt),
        in_specs=[
            pl.BlockSpec((N2 * SUB, TD_F), lambda b, kc, dt: (b * nkc + kc, dt)),
            pl.BlockSpec((N2 * SUB, TD_F), lambda b, kc, dt: (b * nkc + kc, dt)),
            pl.BlockSpec((2 * N2 * SUB, 2 * N2 * SUB), lambda b, kc, dt: (0, 0)),
            pl.BlockSpec((2 * FC, FC), lambda b, kc, dt: (0, 0)),
        ],
        out_specs=pl.BlockSpec((N2, SUB, TD_F), lambda b, kc, dt: (b, kc, dt)),
        out_shape=jax.ShapeDtypeStruct((nb * s // n1, n1, D), F32),
        compiler_params=_cp(("parallel", "parallel", "parallel")),
        name="fnet_stage2",
    )(ar2, ai2, k2, cs)
    mixed = mixed.reshape(nb * s, D)

    mixed_ctx = None
    if with_ctx:
        fc, fs = _dft_mats(c)
        fmat = jnp.asarray(np.concatenate([fc, -fs], axis=0), BF16)
        cblk = nb * s // c
        mixed_ctx = pl.pallas_call(
            _fnet_ctx_kernel,
            grid=(nb,),
            in_specs=[
                pl.BlockSpec((c, D), lambda b: (cblk + b, 0)),
                pl.BlockSpec((1, D), lambda b: (0, 0)),
                pl.BlockSpec((None, NMOD, D), lambda b: (nb, 0, 0)),
                pl.BlockSpec((2 * c, c), lambda b: (0, 0)),
                pl.BlockSpec((2 * FC, FC), lambda b: (0, 0)),
            ],
            out_specs=pl.BlockSpec((c, D), lambda b: (b, 0)),
            out_shape=jax.ShapeDtypeStruct((nb * c, D), F32),
            compiler_params=_cp(("arbitrary",)),
            name="fnet_ctx",
        )(x_all, gain, mod, fmat, cs)
    return mixed, mixed_ctx


M_E1, M_E2, M_R1, M_R2, M_W1, M_W2, M_PG = range(7)
LANE_E0 = NG


def _route_kernel(x_ref, g_ref, mod_ref, wr_ref, br_ref, tok_ref, meta_ref, cnt_ref, base_scr):
    i = pl.program_id(0)
    tm = x_ref.shape[0]

    @pl.when(i == 0)
    def _():
        base_scr[...] = jnp.zeros(base_scr.shape, F32)

    tok = _rms(x_ref[...]) * g_ref[...]
    tok = tok * (1.0 + mod_ref[4:5, :]) + mod_ref[3:4, :]
    tok_ref[...] = tok
    logits = jnp.dot(tok, wr_ref[...], precision=lax.Precision.HIGHEST, preferred_element_type=F32) + br_ref[...]

    lane = lax.broadcasted_iota(jnp.int32, logits.shape, 1)
    big = jnp.int32(LANES)
    gmask = lane < NG
    gl = jnp.where(gmask, logits, NEG)
    gmax = jnp.max(gl, axis=-1, keepdims=True)
    gsum = jnp.sum(jnp.where(gmask, jnp.exp(gl - gmax), 0.0), axis=-1, keepdims=True)
    gidx = jnp.min(jnp.where(gl == gmax, lane, big), axis=-1, keepdims=True)
    lo = LANE_E0 + gidx * EPG
    emask = (lane >= lo) & (lane < lo + EPG)
    el = jnp.where(emask, logits, NEG)
    emax = jnp.max(el, axis=-1, keepdims=True)
    idx1 = jnp.min(jnp.where(el == emax, lane, big), axis=-1, keepdims=True)
    el2 = jnp.where(lane == idx1, NEG, el)
    e2max = jnp.max(el2, axis=-1, keepdims=True)
    idx2 = jnp.min(jnp.where(el2 == e2max, lane, big), axis=-1, keepdims=True)
    t = jnp.exp(e2max - emax)
    w1 = 1.0 / (1.0 + t)
    w2 = t / (1.0 + t)

    oh1 = (lane == idx1).astype(F32)
    oh2 = (lane == idx2).astype(F32)
    rr = lax.broadcasted_iota(jnp.int32, (tm, tm), 0)
    cc = lax.broadcasted_iota(jnp.int32, (tm, tm), 1)
    tri = (cc < rr).astype(BF16)
    base = base_scr[0:1, :]
    cnt1 = jnp.sum(oh1, axis=0, keepdims=True)
    cnt2 = jnp.sum(oh2, axis=0, keepdims=True)
    pre1 = _dot(tri, oh1.astype(BF16)) + base
    pre2 = _dot(tri, oh2.astype(BF16)) + base + cnt1
    r1 = jnp.sum(oh1 * pre1, axis=-1, keepdims=True)
    r2 = jnp.sum(oh2 * pre2, axis=-1, keepdims=True)
    new_base = base + cnt1 + cnt2
    base_scr[0:1, :] = new_base
    cnt_ref[...] = jnp.broadcast_to(new_base, cnt_ref.shape)

    rec = jnp.zeros(logits.shape, F32)
    for k, val in ((M_E1, (idx1 - LANE_E0).astype(F32)), (M_E2, (idx2 - LANE_E0).astype(F32)),
                   (M_R1, r1), (M_R2, r2), (M_W1, w1), (M_W2, w2), (M_PG, 1.0 / gsum)):
        rec = jnp.where(lane == k, val, rec)
    meta_ref[...] = rec


def _expert_kernel(te_ref, nv_ref, src_ref, tok_hbm, wg_ref, wu_ref, wd_ref, o_ref, xbuf, wg_s, wu_s, wd_s, sem):
    t = pl.program_id(0)

    @pl.when(t < nv_ref[0])
    def _():
        def issue(rw, carry):
            pltpu.make_async_copy(tok_hbm.at[pl.ds(src_ref[0, rw], 1)], xbuf.at[pl.ds(rw, 1)], sem).start()
            return carry

        lax.fori_loop(0, TMX, issue, 0)
        prev = te_ref[jnp.maximum(t - 1, 0)]

        @pl.when((t == 0) | (te_ref[t] != prev))
        def _():
            wg_s[...] = wg_ref[...].astype(BF16)
            wu_s[...] = wu_ref[...].astype(BF16)
            wd_s[...] = wd_ref[...].astype(BF16)

        pltpu.make_async_copy(tok_hbm.at[pl.ds(0, TMX)], xbuf, sem).wait()
        x = xbuf[...].astype(BF16)
        a = _dot(x, wg_s[...])
        u = _dot(x, wu_s[...])
        hmid = (a * jax.nn.sigmoid(a) * u).astype(BF16)
        o_ref[...] = _dot(hmid, wd_s[...])

    @pl.when(t >= nv_ref[0])
    def _():
        o_ref[...] = jnp.zeros(o_ref.shape, F32)


def _combine_kernel(d_ref, meta_ref, ys_hbm, x_ref, g_ref, mod_ref, o_ref, buf, sem):
    tm = x_ref.shape[0]

    def issue(rw, carry):
        pltpu.make_async_copy(ys_hbm.at[pl.ds(d_ref[0, rw], 1)], buf.at[0, pl.ds(rw, 1)], sem.at[0]).start()
        pltpu.make_async_copy(ys_hbm.at[pl.ds(d_ref[0, tm + rw], 1)], buf.at[1, pl.ds(rw, 1)], sem.at[1]).start()
        return carry

    lax.fori_loop(0, tm, issue, 0)
    pltpu.make_async_copy(ys_hbm.at[pl.ds(0, tm)], buf.at[0], sem.at[0]).wait()
    pltpu.make_async_copy(ys_hbm.at[pl.ds(0, tm)], buf.at[1], sem.at[1]).wait()
    meta = meta_ref[...]
    w1 = meta[:, M_W1:M_W1 + 1]
    w2 = meta[:, M_W2:M_W2 + 1]
    pg = meta[:, M_PG:M_PG + 1]
    y = w1 * buf[0] + w2 * buf[1]
    o_ref[...] = x_ref[...] + mod_ref[5:6, :] * (pg * (_rms(y) * g_ref[...]))


def _moe(x_all, layer, pre_gain, post_gain, mod, wr, br, w_gate, w_up, w_down, tpb, nb, nrows, out_rows):
    tm = TM_MOE
    nt = nrows // tm
    mod_map = lambda i: (jnp.minimum(i // tpb, nb), 0, 0)
    tok, meta, cnt = pl.pallas_call(
        _route_kernel,
        grid=(nt,),
        in_specs=[
            pl.BlockSpec((tm, D), lambda i: (i, 0)),
            pl.BlockSpec((1, D), lambda i: (0, 0)),
            pl.BlockSpec((None, NMOD, D), mod_map),
            pl.BlockSpec((D, LANES), lambda i: (0, 0)),
            pl.BlockSpec((1, LANES), lambda i: (0, 0)),
        ],
        out_specs=[
            pl.BlockSpec((tm, D), lambda i: (i, 0)),
            pl.BlockSpec((tm, LANES), lambda i: (i, 0)),
            pl.BlockSpec((SUB, LANES), lambda i: (0, 0)),
        ],
        out_shape=[
            jax.ShapeDtypeStruct((nrows, D), F32),
            jax.ShapeDtypeStruct((nrows, LANES), F32),
            jax.ShapeDtypeStruct((SUB, LANES), F32),
        ],
        scratch_shapes=[pltpu.VMEM((SUB, LANES), F32)],
        compiler_params=_cp(("arbitrary",)),
        name="moe_route",
    )(x_all, pre_gain, mod, wr, br)

    ntile = nrows * 2 // TMX + NE
    nslot = ntile * TMX
    e = meta[:, M_E1:M_E2 + 1].astype(jnp.int32)
    rk = meta[:, M_R1:M_R2 + 1].astype(jnp.int32)
    counts = cnt[0, LANE_E0:LANE_E0 + NE].astype(jnp.int32)
    padded = (counts + TMX - 1) // TMX * TMX
    seg_end = jnp.cumsum(padded)
    seg_start = seg_end - padded
    dest = seg_start[e] + rk
    tokid = jnp.broadcast_to(jnp.arange(nrows, dtype=jnp.int32)[:, None], dest.shape)
    src = jnp.zeros((nslot,), jnp.int32).at[dest.reshape(-1)].set(tokid.reshape(-1))
    nvalid = (seg_end[-1] // TMX).astype(jnp.int32).reshape(1)
    tile_start = jnp.arange(ntile, dtype=jnp.int32) * TMX
    te = jnp.searchsorted(seg_end, tile_start, side="right").astype(jnp.int32)
    te = jnp.minimum(te, te[jnp.maximum(nvalid[0] - 1, 0)])

    ys = pl.pallas_call(
        _expert_kernel,
        grid_spec=pltpu.PrefetchScalarGridSpec(
            num_scalar_prefetch=2,
            grid=(ntile,),
            in_specs=[
                pl.BlockSpec((None, 1, TMX), lambda t, te_r, nv_r: (t, 0, 0), memory_space=pltpu.SMEM),
                pl.BlockSpec(memory_space=pl.ANY),
                pl.BlockSpec((None, None, D, FF), lambda t, te_r, nv_r: (layer, te_r[t], 0, 0)),
                pl.BlockSpec((None, None, D, FF), lambda t, te_r, nv_r: (layer, te_r[t], 0, 0)),
                pl.BlockSpec((None, None, FF, D), lambda t, te_r, nv_r: (layer, te_r[t], 0, 0)),
            ],
            out_specs=pl.BlockSpec((TMX, D), lambda t, te_r, nv_r: (t, 0)),
            scratch_shapes=[
                pltpu.VMEM((TMX, D), F32),
                pltpu.VMEM((D, FF), BF16),
                pltpu.VMEM((D, FF), BF16),
                pltpu.VMEM((FF, D), BF16),
                pltpu.SemaphoreType.DMA(()),
            ],
        ),
        out_shape=jax.ShapeDtypeStruct((nslot, D), F32),
        compiler_params=_cp(("arbitrary",)),
        name="moe_experts",
    )(te, nvalid, src.reshape(ntile, 1, TMX), tok, w_gate, w_up, w_down)

    dtile = dest.reshape(nt, tm, 2).transpose(0, 2, 1).reshape(nt, 1, 2 * tm)
    alias = {} if out_rows != x_all.shape[0] else {3: 0}
    return pl.pallas_call(
        _combine_kernel,
        grid=(nt,),
        in_specs=[
            pl.BlockSpec((None, 1, 2 * tm), lambda i: (i, 0, 0), memory_space=pltpu.SMEM),
            pl.BlockSpec((tm, LANES), lambda i: (i, 0)),
            pl.BlockSpec(memory_space=pl.ANY),
            pl.BlockSpec((tm, D), lambda i: (i, 0)),
            pl.BlockSpec((1, D), lambda i: (0, 0)),
            pl.BlockSpec((None, NMOD, D), mod_map),
        ],
        out_specs=pl.BlockSpec((tm, D), lambda i: (i, 0)),
        out_shape=jax.ShapeDtypeStruct((out_rows, D), F32),
        scratch_shapes=[pltpu.VMEM((2, tm, D), F32), pltpu.SemaphoreType.DMA((2,))],
        input_output_aliases=alias,
        compiler_params=_cp(("arbitrary",)),
        name="moe_combine",
    )(dtile, meta, ys, x_all, post_gain, mod)


def _rope_tables(nb, s, c):
    pos = jnp.arange(s, dtype=jnp.int32)
    row = (pos // GRID_W).astype(F32)
    colp = (pos % GRID_W).astype(F32)
    inv_freq = ROPE_THETA ** (-jnp.arange(ROT_FREQS, dtype=F32) / ROT_FREQS)
    ar = row[:, None] * inv_freq
    ac = colp[:, None] * inv_freq
    cos = jnp.concatenate([jnp.cos(ar), jnp.cos(ar), jnp.cos(ac), jnp.cos(ac)], axis=1)
    sin = jnp.concatenate([-jnp.sin(ar), jnp.sin(ar), -jnp.sin(ac), jnp.sin(ac)], axis=1)
    cos = jnp.concatenate([jnp.tile(cos, (nb, 1)), jnp.ones((nb * c, HD), F32)], axis=0)
    sin = jnp.concatenate([jnp.tile(sin, (nb, 1)), jnp.zeros((nb * c, HD), F32)], axis=0)
    return cos, sin


def kernel(x, c, ctx, c_ctx, pre_norm_mix, post_norm_mix, pre_norm_ffn, post_norm_ffn, w_mod, b_mod, fnet_w_out, gqa_w_qkv, gqa_q_norm, gqa_k_norm, gqa_w_o, swa_w_qkv, swa_sink, swa_w_o, router_group_w, router_group_b, router_expert_w, router_expert_b, expert_w_gate, expert_w_up, expert_w_down):
    nb, s, _ = x.shape
    cl = ctx.shape[1]
    depth = w_mod.shape[0]
    r_lat = nb * s
    r_all = r_lat + nb * cl
    assert s % TQ_FLASH == 0 and s % (N2 * SUB) == 0 and (nb * cl) % TM_QKV == 0 and r_all % (s // N2) == 0

    x_all = jnp.concatenate([x.reshape(r_lat, D), ctx.reshape(nb * cl, D)], axis=0)
    cvec = jnp.zeros((SUB, D), F32).at[:nb].set(c).at[nb].set(c_ctx)
    mods = _modulation(cvec, w_mod, b_mod)[:, :nb + 1].reshape(depth, nb + 1, NMOD, D)
    cos_e, sin_e = _rope_tables(nb, s, cl)

    for i in range(depth):
        kind = i % 3
        j = i // 3
        ctx_out = i < depth - 1
        mod = mods[i]
        g_pre = pre_norm_mix[i].reshape(1, D)
        g_post = post_norm_mix[i].reshape(1, D)

        if kind == 0:
            mixed, mixed_ctx = _fourier_mix(x_all, g_pre, mod, nb, s, cl, ctx_out)
            w_out = fnet_w_out[j].astype(BF16)
        else:
            if kind == 1:
                w_qkv, w_out = gqa_w_qkv[j], gqa_w_o[j].astype(BF16)
                qg, kg = gqa_q_norm[j], gqa_k_norm[j]
            else:
                w_qkv, w_out = swa_w_qkv[j], swa_w_o[j].astype(BF16)
                qg = kg = jnp.ones((HD,), F32)
            head_gain = jnp.concatenate([jnp.tile(qg * ATTN_SCALE, NH), jnp.tile(kg, NKV), jnp.ones((NKV * HD,), F32)])
            head_gain = head_gain.reshape(QKV // TN_QKV, 1, TN_QKV)
            qkv = _qkv_proj(x_all, g_pre, mod, w_qkv.astype(BF16), head_gain, cos_e, sin_e,
                            s // TM_QKV, nb, do_norm=(kind == 1))
            if kind == 1:
                mixed = _global_attention(qkv, nb, s, cl)
                sink = jnp.zeros((NH,), F32)
            else:
                sink = swa_sink[j]
                mixed = _window_attention(qkv, sink, nb, s, cl)
            mixed_ctx = _ctx_attention(qkv, sink, nb, s, cl, use_sink=(kind == 2)) if ctx_out else None

        rows = r_all if ctx_out else r_lat
        x_all = _oproj(mixed, mixed_ctx if ctx_out else mixed, w_out, x_all, g_post, mod,
                       s // TM_OPROJ, nb, ctx_out)

        wr = jnp.zeros((D, LANES), F32).at[:, :NG].set(router_group_w[i]).at[:, LANE_E0:LANE_E0 + NE].set(router_expert_w[i])
        br = jnp.zeros((1, LANES), F32).at[0, :NG].set(router_group_b[i]).at[0, LANE_E0:LANE_E0 + NE].set(router_expert_b[i])
        x_all = _moe(x_all, i, pre_norm_ffn[i].reshape(1, D), post_norm_ffn[i].reshape(1, D), mod, wr, br,
                     expert_w_gate, expert_w_up, expert_w_down, s // TM_MOE, nb, rows,
                     r_all if ctx_out else r_lat)
    return x_all.reshape(nb, s, D)
```

```python
import functools
import math

import numpy as np
import jax
import jax.numpy as jnp
from jax import lax
from jax.experimental import pallas as pl
from jax.experimental.pallas import tpu as pltpu

F32 = jnp.float32
BF16 = jnp.bfloat16

D = 2048
HD = 128
NH = 16
NKV = 4
GQ = NH // NKV
QKV = (NH + 2 * NKV) * HD
GRID_W = 64
ROPE_THETA = 10000.0
ROT_FREQS = HD // 4
WINDOW = 128
FG = 8
FC = D // FG
NG = 4
EPG = 8
NE = NG * EPG
FF = 512
NMOD = 6
EPS = 1e-6
ATTN_SCALE = HD ** -0.5
DEPTH = 4

LANES = 128
SUB = 8
VMEM_LIMIT = 56 * 1024 * 1024

TM_QKV = 512
TN_QKV = 256
TM_OPROJ = 256
TQ_FLASH = 1024
TK_FLASH = 1024
FLASH_MARGIN = 60.0
LOG2E = math.log2(math.e)
TQ_WIN = 256
TM_MOE = 256
TMX = 256
TD_F = 512
N2 = 128
NEG = -1e30


def _cp(sem, vmem=VMEM_LIMIT):
    return pltpu.CompilerParams(dimension_semantics=sem, vmem_limit_bytes=vmem)


def _dot(a, b):
    return jnp.dot(a, b, preferred_element_type=F32)


def _dot_nt(a, b):
    return lax.dot_general(a, b, (((1,), (1,)), ((), ())), preferred_element_type=F32)


def _rms(x):
    return x * lax.rsqrt(jnp.mean(x * x, axis=-1, keepdims=True) + EPS)


def _mod_kernel(c_ref, w_ref, b_ref, o_ref):
    a = c_ref[...]
    a = a * jax.nn.sigmoid(a)
    o_ref[...] = _dot(a.astype(BF16), w_ref[...].astype(BF16)) + b_ref[...]


def _modulation(cvec, w_mod, b_mod):
    depth, _, nm = w_mod.shape
    tn = 1024
    return pl.pallas_call(
        _mod_kernel,
        grid=(depth, nm // tn),
        in_specs=[
            pl.BlockSpec((SUB, D), lambda l, j: (0, 0)),
            pl.BlockSpec((None, D, tn), lambda l, j: (l, 0, j)),
            pl.BlockSpec((None, 1, tn), lambda l, j: (l, 0, j)),
        ],
        out_specs=pl.BlockSpec((None, SUB, tn), lambda l, j: (l, 0, j)),
        out_shape=jax.ShapeDtypeStruct((depth, SUB, nm), F32),
        compiler_params=_cp(("arbitrary", "arbitrary")),
        name="modulation",
    )(cvec, w_mod, b_mod.reshape(depth, 1, nm))


def _qkv_kernel(x_ref, g_ref, mod_ref, w_ref, hg_ref, cos_ref, sin_ref, o_ref, *, do_norm, n_rot):
    h = _rms(x_ref[...]) * g_ref[...]
    h = (h * (1.0 + mod_ref[1:2, :]) + mod_ref[0:1, :]).astype(BF16)
    cos = cos_ref[...]
    sin = sin_ref[...]
    lane = lax.broadcasted_iota(jnp.int32, cos.shape, 1)
    first = (lane % (2 * ROT_FREQS)) < ROT_FREQS
    for j in range(QKV // TN_QKV):
        cols = slice(j * TN_QKV, (j + 1) * TN_QKV)
        y = _dot(h, w_ref[:, cols])
        if j >= n_rot:
            o_ref[:, cols] = y.astype(BF16)
            continue
        outs = []
        for hh in range(TN_QKV // HD):
            yh = y[:, hh * HD:(hh + 1) * HD]
            if do_norm:
                yh = _rms(yh)
            yh = yh * hg_ref[:, j * TN_QKV + hh * HD:j * TN_QKV + (hh + 1) * HD]
            partner = jnp.where(first, pltpu.roll(yh, HD - ROT_FREQS, 1), pltpu.roll(yh, ROT_FREQS, 1))
            outs.append(yh * cos + partner * sin)
        o_ref[:, cols] = jnp.concatenate(outs, axis=1).astype(BF16)


def _qkv_proj(x_all, gain, mod, w_bf, head_gain, cos_e, sin_e, tpb, nb, do_norm):
    r = x_all.shape[0]
    n_rot = (NH + NKV) * HD // TN_QKV
    return pl.pallas_call(
        functools.partial(_qkv_kernel, do_norm=do_norm, n_rot=n_rot),
        grid=(r // TM_QKV,),
        in_specs=[
            pl.BlockSpec((TM_QKV, D), lambda i: (i, 0)),
            pl.BlockSpec((1, D), lambda i: (0, 0)),
            pl.BlockSpec((None, NMOD, D), lambda i: (jnp.minimum(i // tpb, nb), 0, 0)),
            pl.BlockSpec((D, QKV), lambda i: (0, 0)),
            pl.BlockSpec((1, QKV), lambda i: (0, 0)),
            pl.BlockSpec((TM_QKV, HD), lambda i: (i, 0)),
            pl.BlockSpec((TM_QKV, HD), lambda i: (i, 0)),
        ],
        out_specs=pl.BlockSpec((TM_QKV, QKV), lambda i: (i, 0)),
        out_shape=jax.ShapeDtypeStruct((r, QKV), BF16),
        compiler_params=_cp(("parallel",)),
        name="qkv_proj",
    )(x_all, gain, mod, w_bf, head_gain, cos_e, sin_e)


def _flash_kernel(q_ref, kc_ref, vc_ref, k_ref, v_ref, o_ref, m_scr, acc_scr):
    j = pl.program_id(3)

    def chunks(s):
        return [s[:, c * LANES:(c + 1) * LANES] for c in range(s.shape[1] // LANES)]

    def rebase_update(k, v1):
        for g in range(GQ):
            ch = chunks(_dot_nt(q_ref[:, g * HD:(g + 1) * HD], k))
            m_prev = m_scr[g]
            m_new = jnp.maximum(m_prev, jnp.max(functools.reduce(jnp.maximum, ch), axis=-1, keepdims=True))
            alpha = jnp.exp2(m_prev - m_new)
            p = jnp.concatenate([jnp.exp2(c - m_new).astype(BF16) for c in ch], axis=1)
            acc = acc_scr[g]
            acc_scr[g] = jnp.concatenate([acc[:, :HD] * alpha, acc[:, HD:] * alpha], axis=1) + _dot(p, v1)
            m_scr[g] = m_new

    @pl.when(j == 0)
    def _():
        m_scr[...] = jnp.full(m_scr.shape, NEG, F32)
        acc_scr[...] = jnp.zeros(acc_scr.shape, F32)
        vc = vc_ref[...]
        rebase_update(kc_ref[...], jnp.concatenate([vc, jnp.ones_like(vc)], axis=1))

    k = k_ref[...]
    v = v_ref[...]
    v1 = jnp.concatenate([v, jnp.ones_like(v)], axis=1)
    ps = []
    excess = None
    for g in range(GQ):
        ch = chunks(_dot_nt(q_ref[:, g * HD:(g + 1) * HD], k))
        m = m_scr[g]
        over = functools.reduce(jnp.maximum, ch) - m
        excess = over if excess is None else jnp.maximum(excess, over)
        ps.append(jnp.concatenate([jnp.exp2(c - m).astype(BF16) for c in ch], axis=1))
    worst = jnp.max(jnp.max(excess, axis=0, keepdims=True), axis=1, keepdims=True)[0, 0]

    @pl.when(worst <= FLASH_MARGIN)
    def _():
        for g in range(GQ):
            acc_scr[g] += _dot(ps[g], v1)

    @pl.when(worst > FLASH_MARGIN)
    def _():
        rebase_update(k, v1)

    @pl.when(j == pl.num_programs(3) - 1)
    def _():
        for g in range(GQ):
            acc = acc_scr[g]
            o_ref[:, g * HD:(g + 1) * HD] = (acc[:, :HD] / acc[:, HD:]).astype(BF16)


def _global_attention(qkv, nb, s, c):
    nq = s // TQ_FLASH
    nk = s // TK_FLASH
    kcol = NH
    vcol = NH + NKV
    cblk = nb * s // c
    return pl.pallas_call(
        _flash_kernel,
        grid=(nb, NKV, nq, nk),
        in_specs=[
            pl.BlockSpec((TQ_FLASH, GQ * HD), lambda b, h, i, j: (b * nq + i, h)),
            pl.BlockSpec((c, HD), lambda b, h, i, j: (cblk + b, kcol + h)),
            pl.BlockSpec((c, HD), lambda b, h, i, j: (cblk + b, vcol + h)),
            pl.BlockSpec((TK_FLASH, HD), lambda b, h, i, j: (b * nk + j, kcol + h)),
            pl.BlockSpec((TK_FLASH, HD), lambda b, h, i, j: (b * nk + j, vcol + h)),
        ],
        out_specs=pl.BlockSpec((TQ_FLASH, GQ * HD), lambda b, h, i, j: (b * nq + i, h)),
        out_shape=jax.ShapeDtypeStruct((nb * s, NH * HD), BF16),
        scratch_shapes=[
            pltpu.VMEM((GQ, TQ_FLASH, LANES), F32),
            pltpu.VMEM((GQ, TQ_FLASH, 2 * HD), F32),
        ],
        compiler_params=_cp(("parallel", "parallel", "parallel", "arbitrary")),
        name="global_attention",
    )(qkv, qkv, qkv, qkv, qkv)


def _window_kernel(sink_ref, q_ref, kc_ref, vc_ref, kp_ref, vp_ref, k0_ref, v0_ref, kn_ref, vn_ref, o_ref, *, s_len):
    h = pl.program_id(1)
    i = pl.program_id(2)
    tq = q_ref.shape[0]
    qpos = i * tq + lax.broadcasted_iota(jnp.int32, (tq, tq), 0)
    col = lax.broadcasted_iota(jnp.int32, (tq, tq), 1)
    segs = [(kc_ref, vc_ref, None)]
    for off, kr, vr in ((-1, kp_ref, vp_ref), (0, k0_ref, v0_ref), (1, kn_ref, vn_ref)):
        kpos = (i + off) * tq + col
        valid = (jnp.abs(qpos - kpos) <= WINDOW) & (kpos >= 0) & (kpos < s_len)
        segs.append((kr, vr, valid))
    for g in range(GQ):
        q = q_ref[:, g * HD:(g + 1) * HD]
        sink = sink_ref[h * GQ + g] * LOG2E
        scores = []
        m = jnp.full((tq, 1), sink, F32)
        for kr, _, valid in segs:
            sc = _dot_nt(q, kr[...])
            if valid is not None:
                sc = jnp.where(valid, sc, NEG)
            scores.append(sc)
            m = jnp.maximum(m, jnp.max(sc, axis=-1, keepdims=True))
        l = jnp.exp2(sink - m)
        acc = jnp.zeros((tq, HD), F32)
        for sc, (_, vr, valid) in zip(scores, segs):
            p = jnp.exp2(sc - m)
            if valid is not None:
                p = jnp.where(valid, p, 0.0)
            l = l + jnp.sum(p, axis=-1, keepdims=True)
            acc = acc + _dot(p.astype(BF16), vr[...])
        o_ref[:, g * HD:(g + 1) * HD] = (acc / l).astype(BF16)


def _window_attention(qkv, sink, nb, s, c):
    tq = TQ_WIN
    nq = s // tq
    kcol = NH
    vcol = NH + NKV
    cblk = nb * s // c

    def kv(col, off):
        return pl.BlockSpec((tq, HD), lambda b, h, i: (b * nq + jnp.clip(i + off, 0, nq - 1), col + h))

    return pl.pallas_call(
        functools.partial(_window_kernel, s_len=s),
        grid=(nb, NKV, nq),
        in_specs=[
            pl.BlockSpec(memory_space=pltpu.SMEM),
            pl.BlockSpec((tq, GQ * HD), lambda b, h, i: (b * nq + i, h)),
            pl.BlockSpec((c, HD), lambda b, h, i: (cblk + b, kcol + h)),
            pl.BlockSpec((c, HD), lambda b, h, i: (cblk + b, vcol + h)),
            kv(kcol, -1), kv(vcol, -1), kv(kcol, 0), kv(vcol, 0), kv(kcol, 1), kv(vcol, 1),
        ],
        out_specs=pl.BlockSpec((tq, GQ * HD), lambda b, h, i: (b * nq + i, h)),
        out_shape=jax.ShapeDtypeStruct((nb * s, NH * HD), BF16),
        compiler_params=_cp(("parallel", "parallel", "parallel")),
        name="window_attention",
    )(sink, qkv, qkv, qkv, qkv, qkv, qkv, qkv, qkv, qkv)


def _ctx_attn_kernel(sink_ref, q_ref, k_ref, v_ref, o_ref, *, use_sink):
    h = pl.program_id(1)
    k = k_ref[...]
    v = v_ref[...]
    for g in range(GQ):
        s = _dot_nt(q_ref[:, g * HD:(g + 1) * HD], k)
        m = jnp.max(s, axis=-1, keepdims=True)
        if use_sink:
            sink = sink_ref[h * GQ + g] * LOG2E
            m = jnp.maximum(m, sink)
        p = jnp.exp2(s - m)
        l = jnp.sum(p, axis=-1, keepdims=True)
        if use_sink:
            l = l + jnp.exp2(sink - m)
        o_ref[:, g * HD:(g + 1) * HD] = (_dot(p.astype(BF16), v) / l).astype(BF16)


def _ctx_attention(qkv, sink, nb, s, c, use_sink):
    cblk = nb * s // c
    return pl.pallas_call(
        functools.partial(_ctx_attn_kernel, use_sink=use_sink),
        grid=(nb, NKV),
        in_specs=[
            pl.BlockSpec(memory_space=pltpu.SMEM),
            pl.BlockSpec((c, GQ * HD), lambda b, h: (cblk + b, h)),
            pl.BlockSpec((c, HD), lambda b, h: (cblk + b, NH + h)),
            pl.BlockSpec((c, HD), lambda b, h: (cblk + b, NH + NKV + h)),
        ],
        out_specs=pl.BlockSpec((c, GQ * HD), lambda b, h: (b, h)),
        out_shape=jax.ShapeDtypeStruct((nb * c, NH * HD), BF16),
        compiler_params=_cp(("arbitrary", "arbitrary")),
        name="ctx_attention",
    )(sink, qkv, qkv, qkv)


def _oproj_kernel(a_ref, ac_ref, w_ref, x_ref, g_ref, mod_ref, o_ref, *, nlat):
    def project(src_ref):
        y = _dot(src_ref[...].astype(BF16), w_ref[...])
        o_ref[...] = x_ref[...] + mod_ref[2:3, :] * (_rms(y) * g_ref[...])

    i = pl.program_id(0)

    @pl.when(i < nlat)
    def _():
        project(a_ref)

    @pl.when(i >= nlat)
    def _():
        project(ac_ref)


def _oproj(a_lat, a_ctx, w_bf, x_all, gain, mod, tpb, nb, with_ctx):
    tm = TM_OPROJ
    nlat = a_lat.shape[0] // tm
    ntiles = nlat + (a_ctx.shape[0] // tm if with_ctx else 0)
    return pl.pallas_call(
        functools.partial(_oproj_kernel, nlat=nlat),
        grid=(ntiles,),
        in_specs=[
            pl.BlockSpec((tm, D), lambda i: (jnp.minimum(i, nlat - 1), 0)),
            pl.BlockSpec((tm, D), lambda i: (jnp.maximum(i - nlat, 0), 0)),
            pl.BlockSpec((D, D), lambda i: (0, 0)),
            pl.BlockSpec((tm, D), lambda i: (i, 0)),
            pl.BlockSpec((1, D), lambda i: (0, 0)),
            pl.BlockSpec((None, NMOD, D), lambda i: (jnp.minimum(i // tpb, nb), 0, 0)),
        ],
        out_specs=pl.BlockSpec((tm, D), lambda i: (i, 0)),
        out_shape=jax.ShapeDtypeStruct(x_all.shape, F32),
        input_output_aliases={3: 0},
        compiler_params=_cp(("parallel",)),
        name="oproj",
    )(a_lat, a_ctx, w_bf, x_all, gain, mod)


def _dft_mats(n):
    k = np.arange(n)
    ang = 2.0 * np.pi * ((k[:, None] * k[None, :]) % n) / n
    return np.cos(ang) / math.sqrt(n), np.sin(ang) / math.sqrt(n)


def _stage1_matrix(n1):
    c, s = _dft_mats(n1)
    eye = np.eye(SUB)
    return np.concatenate([np.kron(c, eye), -np.kron(s, eye)], axis=0)


def _stage2_matrix():
    c, s = _dft_mats(N2)
    eye = np.eye(SUB)
    pc = np.einsum("kn,ab->kabn", c, eye).reshape(N2 * SUB, SUB * N2)
    ps = np.einsum("kn,ab->kabn", s, eye).reshape(N2 * SUB, SUB * N2)
    return np.block([[pc, ps], [-ps, pc]])


def _channel_matrix():
    c, s = _dft_mats(FC)
    return np.concatenate([c, s], axis=0)


def _fft1_kernel(x_ref, g_ref, mod_ref, kr_ref, tc_ref, ts_ref, ar_ref, ai_ref, rstd_scr):
    dt = pl.program_id(2)
    n1 = x_ref.shape[0]
    rows = n1 * SUB

    @pl.when(dt == 0)
    def _():
        x = x_ref[...].reshape(rows, D)
        rstd_scr[...] = lax.rsqrt(jnp.mean(x * x, axis=-1, keepdims=True) + EPS)

    col = pl.multiple_of(dt * TD_F, TD_F)
    x = x_ref[:, :, pl.ds(col, TD_F)].reshape(rows, TD_F)
    h = x * rstd_scr[...] * g_ref[...]
    h = h * (1.0 + mod_ref[1:2, :]) + mod_ref[0:1, :]
    y = _dot(kr_ref[...], h.astype(BF16))
    yr = y[:rows]
    yi = y[rows:]
    tc = tc_ref[...]
    ts = ts_ref[...]
    for cc in range(TD_F // LANES):
        sl = slice(cc * LANES, (cc + 1) * LANES)
        a = yr[:, sl]
        b = yi[:, sl]
        ar_ref[:, :, sl] = (a * tc + b * ts).reshape(n1, SUB, LANES)
        ai_ref[:, :, sl] = (b * tc - a * ts).reshape(n1, SUB, LANES)


def _fft2_kernel(ar_ref, ai_ref, k2_ref, cs_ref, o_ref):
    a = jnp.concatenate([ar_ref[...].astype(BF16), ai_ref[...].astype(BF16)], axis=0)
    y = _dot(k2_ref[...], a)
    rows = N2 * SUB
    yr = y[:rows].astype(BF16)
    yi = y[rows:].astype(BF16)
    for gi in range(TD_F // FC):
        sl = slice(gi * FC, (gi + 1) * FC)
        z = jnp.concatenate([yr[:, sl], yi[:, sl]], axis=1)
        o_ref[:, :, sl] = _dot(z, cs_ref[...]).reshape(N2, SUB, FC)


def _fnet_ctx_kernel(x_ref, g_ref, mod_ref, f_ref, cs_ref, o_ref):
    c = x_ref.shape[0]
    h = _rms(x_ref[...]) * g_ref[...]
    h = h * (1.0 + mod_ref[1:2, :]) + mod_ref[0:1, :]
    y = _dot(f_ref[...], h.astype(BF16))
    yr = y[:c].astype(BF16)
    yi = y[c:].astype(BF16)
    for gi in range(FG):
        sl = slice(gi * FC, (gi + 1) * FC)
        z = jnp.concatenate([yr[:, sl], yi[:, sl]], axis=1)
        o_ref[:, sl] = _dot(z, cs_ref[...])


def _fourier_mix(x_all, gain, mod, nb, s, c, with_ctx):
    r = x_all.shape[0]
    n1 = s // N2
    nch = N2 // SUB
    ndt = D // TD_F
    rows = n1 * SUB
    kr = jnp.asarray(_stage1_matrix(n1), BF16)
    k2 = jnp.asarray(_stage2_matrix(), BF16)
    cs = jnp.asarray(_channel_matrix(), BF16)
    k1 = jnp.arange(n1, dtype=jnp.int32)[None, :, None]
    n2 = (jnp.arange(nch, dtype=jnp.int32)[:, None, None] * SUB + jnp.arange(SUB, dtype=jnp.int32)[None, None, :])
    ang = (2.0 * np.pi / s) * ((k1 * n2) % s).astype(F32).reshape(nch, rows, 1)
    tc = jnp.broadcast_to(jnp.cos(ang), (nch, rows, LANES))
    ts = jnp.broadcast_to(jnp.sin(ang), (nch, rows, LANES))

    x4 = x_all.reshape(r // N2, N2, D)
    ar, ai = pl.pallas_call(
        _fft1_kernel,
        grid=(nb, nch, ndt),
        in_specs=[
            pl.BlockSpec((n1, SUB, D), lambda b, ch, dt: (b, ch, 0)),
            pl.BlockSpec((1, TD_F), lambda b, ch, dt: (0, dt)),
            pl.BlockSpec((None, NMOD, TD_F), lambda b, ch, dt: (b, 0, dt)),
            pl.BlockSpec((2 * rows, rows), lambda b, ch, dt: (0, 0)),
            pl.BlockSpec((None, rows, LANES), lambda b, ch, dt: (ch, 0, 0)),
            pl.BlockSpec((None, rows, LANES), lambda b, ch, dt: (ch, 0, 0)),
        ],
        out_specs=[
            pl.BlockSpec((n1, SUB, TD_F), lambda b, ch, dt: (b, ch, dt)),
            pl.BlockSpec((n1, SUB, TD_F), lambda b, ch, dt: (b, ch, dt)),
        ],
        out_shape=[jax.ShapeDtypeStruct((nb * n1, N2, D), F32)] * 2,
        scratch_shapes=[pltpu.VMEM((rows, 1), F32)],
        compiler_params=_cp(("parallel", "parallel", "arbitrary")),
        name="fnet_stage1",
    )(x4, gain, mod, kr, tc, ts)

    ar2 = ar.reshape(nb * s, D)
    ai2 = ai.reshape(nb * s, D)
    nkc = n1 // SUB
    mixed = pl.pallas_call(
        _fft2_kernel,
        grid=(nb, nkc, ndt),
        in_specs=[
            pl.BlockSpec((N2 * SUB, TD_F), lambda b, kc, dt: (b * nkc + kc, dt)),
            pl.BlockSpec((N2 * SUB, TD_F), lambda b, kc, dt: (b * nkc + kc, dt)),
            pl.BlockSpec((2 * N2 * SUB, 2 * N2 * SUB), lambda b, kc, dt: (0, 0)),
            pl.BlockSpec((2 * FC, FC), lambda b, kc, dt: (0, 0)),
        ],
        out_specs=pl.BlockSpec((N2, SUB, TD_F), lambda b, kc, dt: (b, kc, dt)),
        out_shape=jax.ShapeDtypeStruct((nb * s // n1, n1, D), F32),
        compiler_params=_cp(("parallel", "parallel", "parallel")),
        name="fnet_stage2",
    )(ar2, ai2, k2, cs)
    mixed = mixed.reshape(nb * s, D)

    mixed_ctx = None
    if with_ctx:
        fc, fs = _dft_mats(c)
        fmat = jnp.asarray(np.concatenate([fc, -fs], axis=0), BF16)
        cblk = nb * s // c
        mixed_ctx = pl.pallas_call(
            _fnet_ctx_kernel,
            grid=(nb,),
            in_specs=[
                pl.BlockSpec((c, D), lambda b: (cblk + b, 0)),
                pl.BlockSpec((1, D), lambda b: (0, 0)),
                pl.BlockSpec((None, NMOD, D), lambda b: (nb, 0, 0)),
                pl.BlockSpec((2 * c, c), lambda b: (0, 0)),
                pl.BlockSpec((2 * FC, FC), lambda b: (0, 0)),
            ],
            out_specs=pl.BlockSpec((c, D), lambda b: (b, 0)),
            out_shape=jax.ShapeDtypeStruct((nb * c, D), F32),
            compiler_params=_cp(("arbitrary",)),
            name="fnet_ctx",
        )(x_all, gain, mod, fmat, cs)
    return mixed, mixed_ctx


M_E1, M_E2, M_R1, M_R2, M_W1, M_W2, M_PG = range(7)
LANE_E0 = NG


def _ffn_tokens(x_ref, g_ref, mod_ref):
    tok = _rms(x_ref[...]) * g_ref[...]
    return tok * (1.0 + mod_ref[4:5, :]) + mod_ref[3:4, :]


def _route_kernel(x_ref, g_ref, mod_ref, wr_ref, br_ref, meta_ref, cnt_ref, base_scr):
    i = pl.program_id(0)
    tm = x_ref.shape[0]

    @pl.when(i == 0)
    def _():
        base_scr[...] = jnp.zeros(base_scr.shape, F32)

    tok = _ffn_tokens(x_ref, g_ref, mod_ref)
    logits = jnp.dot(tok, wr_ref[...], precision=lax.Precision.HIGHEST, preferred_element_type=F32) + br_ref[...]

    lane = lax.broadcasted_iota(jnp.int32, logits.shape, 1)
    big = jnp.int32(LANES)
    gmask = lane < NG
    gl = jnp.where(gmask, logits, NEG)
    gmax = jnp.max(gl, axis=-1, keepdims=True)
    gsum = jnp.sum(jnp.where(gmask, jnp.exp(gl - gmax), 0.0), axis=-1, keepdims=True)
    gidx = jnp.min(jnp.where(gl == gmax, lane, big), axis=-1, keepdims=True)
    lo = LANE_E0 + gidx * EPG
    emask = (lane >= lo) & (lane < lo + EPG)
    el = jnp.where(emask, logits, NEG)
    emax = jnp.max(el, axis=-1, keepdims=True)
    idx1 = jnp.min(jnp.where(el == emax, lane, big), axis=-1, keepdims=True)
    el2 = jnp.where(lane == idx1, NEG, el)
    e2max = jnp.max(el2, axis=-1, keepdims=True)
    idx2 = jnp.min(jnp.where(el2 == e2max, lane, big), axis=-1, keepdims=True)
    t = jnp.exp(e2max - emax)
    w1 = 1.0 / (1.0 + t)
    w2 = t / (1.0 + t)

    oh1 = (lane == idx1).astype(F32)
    oh2 = (lane == idx2).astype(F32)
    rr = lax.broadcasted_iota(jnp.int32, (tm, tm), 0)
    cc = lax.broadcasted_iota(jnp.int32, (tm, tm), 1)
    tri = (cc < rr).astype(BF16)
    base = base_scr[0:1, :]
    cnt1 = jnp.sum(oh1, axis=0, keepdims=True)
    cnt2 = jnp.sum(oh2, axis=0, keepdims=True)
    pre1 = _dot(tri, oh1.astype(BF16)) + base
    pre2 = _dot(tri, oh2.astype(BF16)) + base + cnt1
    r1 = jnp.sum(oh1 * pre1, axis=-1, keepdims=True)
    r2 = jnp.sum(oh2 * pre2, axis=-1, keepdims=True)
    new_base = base + cnt1 + cnt2
    base_scr[0:1, :] = new_base
    cnt_ref[...] = jnp.broadcast_to(new_base, cnt_ref.shape)

    rec = jnp.zeros(logits.shape, F32)
    for k, val in ((M_E1, (idx1 - LANE_E0).astype(F32)), (M_E2, (idx2 - LANE_E0).astype(F32)),
                   (M_R1, r1), (M_R2, r2), (M_W1, w1), (M_W2, w2), (M_PG, 1.0 / gsum)):
        rec = jnp.where(lane == k, val, rec)
    meta_ref[...] = rec


def _row_copy_wait(hbm_ref, vmem_ref, sem):
    pltpu.make_async_copy(hbm_ref.at[pl.ds(0, vmem_ref.shape[0])], vmem_ref, sem).wait()


def _dispatch_kernel(pt_ref, d_ref, x_ref, g_ref, mod_ref, xs_hbm, buf, zbuf, sem, zsem):
    i = pl.program_id(0)
    nt = pl.num_programs(0)
    tm = x_ref.shape[0]
    slot = i % 2

    @pl.when(i == 0)
    def _():
        zbuf[...] = jnp.zeros(zbuf.shape, F32)

        def zero_tile(t, carry):
            @pl.when(pt_ref[t] != 0)
            def _():
                pltpu.make_async_copy(zbuf, xs_hbm.at[pl.ds(pl.multiple_of(t * TMX, TMX), TMX)], zsem).start()
            return carry

        def zero_wait(t, carry):
            @pl.when(pt_ref[t] != 0)
            def _():
                _row_copy_wait(xs_hbm, zbuf, zsem)
            return carry

        lax.fori_loop(0, pt_ref.shape[0], zero_tile, 0)
        lax.fori_loop(0, pt_ref.shape[0], zero_wait, 0)

    @pl.when(i >= 2)
    def _():
        _row_copy_wait(xs_hbm, buf.at[slot], sem.at[slot])
        _row_copy_wait(xs_hbm, buf.at[slot], sem.at[slot])

    buf[slot] = _ffn_tokens(x_ref, g_ref, mod_ref)

    def issue(rw, carry):
        src = buf.at[slot, pl.ds(rw, 1)]
        pltpu.make_async_copy(src, xs_hbm.at[pl.ds(d_ref[0, rw], 1)], sem.at[slot]).start()
        pltpu.make_async_copy(src, xs_hbm.at[pl.ds(d_ref[0, tm + rw], 1)], sem.at[slot]).start()
        return carry

    lax.fori_loop(0, tm, issue, 0, unroll=8)

    @pl.when(i == nt - 1)
    def _():
        _row_copy_wait(xs_hbm, buf.at[slot], sem.at[slot])
        _row_copy_wait(xs_hbm, buf.at[slot], sem.at[slot])

        @pl.when(nt >= 2)
        def _():
            _row_copy_wait(xs_hbm, buf.at[1 - slot], sem.at[1 - slot])
            _row_copy_wait(xs_hbm, buf.at[1 - slot], sem.at[1 - slot])


def _expert_kernel(te_ref, nv_ref, xs_ref, wg_ref, wu_ref, wd_ref, o_ref, wg_s, wu_s, wd_s):
    t = pl.program_id(0)

    @pl.when(t < nv_ref[0])
    def _():
        prev = te_ref[jnp.maximum(t - 1, 0)]

        @pl.when((t == 0) | (te_ref[t] != prev))
        def _():
            wg_s[...] = wg_ref[...].astype(BF16)
            wu_s[...] = wu_ref[...].astype(BF16)
            wd_s[...] = wd_ref[...].astype(BF16)

        x = xs_ref[...].astype(BF16)
        a = _dot(x, wg_s[...])
        u = _dot(x, wu_s[...])
        hmid = (a * jax.nn.sigmoid(a) * u).astype(BF16)
        o_ref[...] = _dot(hmid, wd_s[...])

    @pl.when(t >= nv_ref[0])
    def _():
        o_ref[...] = jnp.zeros(o_ref.shape, F32)


def _combine_kernel(d_ref, dn_ref, meta_ref, ys_hbm, x_ref, g_ref, mod_ref, o_ref, buf, sem):
    i = pl.program_id(0)
    nt = pl.num_programs(0)
    tm = x_ref.shape[0]
    slot = i % 2

    def gather(idx_ref, sl):
        def issue(rw, carry):
            pltpu.make_async_copy(ys_hbm.at[pl.ds(idx_ref[0, rw], 1)], buf.at[sl, 0, pl.ds(rw, 1)], sem.at[sl]).start()
            pltpu.make_async_copy(ys_hbm.at[pl.ds(idx_ref[0, tm + rw], 1)], buf.at[sl, 1, pl.ds(rw, 1)], sem.at[sl]).start()
            return carry

        lax.fori_loop(0, tm, issue, 0, unroll=8)

    @pl.when(i == 0)
    def _():
        gather(d_ref, 0)

    @pl.when(i + 1 < nt)
    def _():
        gather(dn_ref, 1 - slot)

    _row_copy_wait(ys_hbm, buf.at[slot, 0], sem.at[slot])
    _row_copy_wait(ys_hbm, buf.at[slot, 1], sem.at[slot])
    meta = meta_ref[...]
    w1 = meta[:, M_W1:M_W1 + 1]
    w2 = meta[:, M_W2:M_W2 + 1]
    pg = meta[:, M_PG:M_PG + 1]
    y = w1 * buf[slot, 0] + w2 * buf[slot, 1]
    o_ref[...] = x_ref[...] + mod_ref[5:6, :] * (pg * (_rms(y) * g_ref[...]))


def _moe(x_all, layer, pre_gain, post_gain, mod, wr, br, w_gate, w_up, w_down, tpb, nb, nrows, out_rows):
    tm = TM_MOE
    nt = nrows // tm
    mod_map = lambda i: (jnp.minimum(i // tpb, nb), 0, 0)
    meta, cnt = pl.pallas_call(
        _route_kernel,
        grid=(nt,),
        in_specs=[
            pl.BlockSpec((tm, D), lambda i: (i, 0)),
            pl.BlockSpec((1, D), lambda i: (0, 0)),
            pl.BlockSpec((None, NMOD, D), mod_map),
            pl.BlockSpec((D, LANES), lambda i: (0, 0)),
            pl.BlockSpec((1, LANES), lambda i: (0, 0)),
        ],
        out_specs=[
            pl.BlockSpec((tm, LANES), lambda i: (i, 0)),
            pl.BlockSpec((SUB, LANES), lambda i: (0, 0)),
        ],
        out_shape=[
            jax.ShapeDtypeStruct((nrows, LANES), F32),
            jax.ShapeDtypeStruct((SUB, LANES), F32),
        ],
        scratch_shapes=[pltpu.VMEM((SUB, LANES), F32)],
        compiler_params=_cp(("arbitrary",)),
        name="moe_route",
    )(x_all, pre_gain, mod, wr, br)

    ntile = nrows * 2 // TMX + NE
    nslot = ntile * TMX
    e = meta[:, M_E1:M_E2 + 1].astype(jnp.int32)
    rk = meta[:, M_R1:M_R2 + 1].astype(jnp.int32)
    counts = cnt[0, LANE_E0:LANE_E0 + NE].astype(jnp.int32)
    padded = (counts + TMX - 1) // TMX * TMX
    seg_end = jnp.cumsum(padded)
    seg_start = seg_end - padded
    dest = seg_start[e] + rk
    dtile = dest.reshape(nt, tm, 2).transpose(0, 2, 1).reshape(nt, 1, 2 * tm)
    nvalid = (seg_end[-1] // TMX).astype(jnp.int32).reshape(1)
    tile_id = jnp.arange(ntile, dtype=jnp.int32)
    te_raw = jnp.sum((tile_id[:, None] * TMX >= seg_end[None, :]).astype(jnp.int32), axis=1)
    te = jnp.minimum(te_raw, te_raw[jnp.maximum(nvalid[0] - 1, 0)])
    last_of_seg = jnp.concatenate([te_raw[1:] != te_raw[:-1], jnp.ones((1,), bool)])
    partial = (last_of_seg | (tile_id >= nvalid[0])).astype(jnp.int32)

    xs = pl.pallas_call(
        _dispatch_kernel,
        grid_spec=pltpu.PrefetchScalarGridSpec(
            num_scalar_prefetch=1,
            grid=(nt,),
            in_specs=[
                pl.BlockSpec((None, 1, 2 * tm), lambda i, pt: (i, 0, 0), memory_space=pltpu.SMEM),
                pl.BlockSpec((tm, D), lambda i, pt: (i, 0)),
                pl.BlockSpec((1, D), lambda i, pt: (0, 0)),
                pl.BlockSpec((None, NMOD, D), lambda i, pt: mod_map(i)),
            ],
            out_specs=pl.BlockSpec(memory_space=pl.ANY),
            scratch_shapes=[
                pltpu.VMEM((2, tm, D), F32),
                pltpu.VMEM((TMX, D), F32),
                pltpu.SemaphoreType.DMA((2,)),
                pltpu.SemaphoreType.DMA(()),
            ],
        ),
        out_shape=jax.ShapeDtypeStruct((nslot, D), F32),
        compiler_params=_cp(("arbitrary",)),
        name="moe_dispatch",
    )(partial, dtile, x_all, pre_gain, mod)

    ys = pl.pallas_call(
        _expert_kernel,
        grid_spec=pltpu.PrefetchScalarGridSpec(
            num_scalar_prefetch=2,
            grid=(ntile,),
            in_specs=[
                pl.BlockSpec((TMX, D), lambda t, te_r, nv_r: (jnp.minimum(t, nv_r[0] - 1), 0)),
                pl.BlockSpec((None, None, D, FF), lambda t, te_r, nv_r: (layer, te_r[t], 0, 0)),
                pl.BlockSpec((None, None, D, FF), lambda t, te_r, nv_r: (layer, te_r[t], 0, 0)),
                pl.BlockSpec((None, None, FF, D), lambda t, te_r, nv_r: (layer, te_r[t], 0, 0)),
            ],
            out_specs=pl.BlockSpec((TMX, D), lambda t, te_r, nv_r: (t, 0)),
            scratch_shapes=[
                pltpu.VMEM((D, FF), BF16),
                pltpu.VMEM((D, FF), BF16),
                pltpu.VMEM((FF, D), BF16),
            ],
        ),
        out_shape=jax.ShapeDtypeStruct((nslot, D), F32),
        compiler_params=_cp(("arbitrary",)),
        name="moe_experts",
    )(te, nvalid, xs, w_gate, w_up, w_down)

    alias = {} if out_rows != x_all.shape[0] else {4: 0}
    return pl.pallas_call(
        _combine_kernel,
        grid=(nt,),
        in_specs=[
            pl.BlockSpec((None, 1, 2 * tm), lambda i: (i, 0, 0), memory_space=pltpu.SMEM),
            pl.BlockSpec((None, 1, 2 * tm), lambda i: (jnp.minimum(i + 1, nt - 1), 0, 0), memory_space=pltpu.SMEM),
            pl.BlockSpec((tm, LANES), lambda i: (i, 0)),
            pl.BlockSpec(memory_space=pl.ANY),
            pl.BlockSpec((tm, D), lambda i: (i, 0)),
            pl.BlockSpec((1, D), lambda i: (0, 0)),
            pl.BlockSpec((None, NMOD, D), mod_map),
        ],
        out_specs=pl.BlockSpec((tm, D), lambda i: (i, 0)),
        out_shape=jax.ShapeDtypeStruct((out_rows, D), F32),
        scratch_shapes=[pltpu.VMEM((2, 2, tm, D), F32), pltpu.SemaphoreType.DMA((2,))],
        input_output_aliases=alias,
        compiler_params=_cp(("arbitrary",)),
        name="moe_combine",
    )(dtile, dtile, meta, ys, x_all, post_gain, mod)


def _rope_tables(nb, s, c):
    pos = jnp.arange(s, dtype=jnp.int32)
    row = (pos // GRID_W).astype(F32)
    colp = (pos % GRID_W).astype(F32)
    inv_freq = ROPE_THETA ** (-jnp.arange(ROT_FREQS, dtype=F32) / ROT_FREQS)
    ar = row[:, None] * inv_freq
    ac = colp[:, None] * inv_freq
    cos = jnp.concatenate([jnp.cos(ar), jnp.cos(ar), jnp.cos(ac), jnp.cos(ac)], axis=1)
    sin = jnp.concatenate([-jnp.sin(ar), jnp.sin(ar), -jnp.sin(ac), jnp.sin(ac)], axis=1)
    cos = jnp.concatenate([jnp.tile(cos, (nb, 1)), jnp.ones((nb * c, HD), F32)], axis=0)
    sin = jnp.concatenate([jnp.tile(sin, (nb, 1)), jnp.zeros((nb * c, HD), F32)], axis=0)
    return cos, sin


def kernel(x, c, ctx, c_ctx, pre_norm_mix, post_norm_mix, pre_norm_ffn, post_norm_ffn, w_mod, b_mod, fnet_w_out, gqa_w_qkv, gqa_q_norm, gqa_k_norm, gqa_w_o, swa_w_qkv, swa_sink, swa_w_o, router_group_w, router_group_b, router_expert_w, router_expert_b, expert_w_gate, expert_w_up, expert_w_down):
    nb, s, _ = x.shape
    cl = ctx.shape[1]
    depth = w_mod.shape[0]
    r_lat = nb * s
    r_all = r_lat + nb * cl
    assert s % TQ_FLASH == 0 and s % (N2 * SUB) == 0 and (nb * cl) % TM_QKV == 0 and r_all % (s // N2) == 0

    x_all = jnp.concatenate([x.reshape(r_lat, D), ctx.reshape(nb * cl, D)], axis=0)
    cvec = jnp.zeros((SUB, D), F32).at[:nb].set(c).at[nb].set(c_ctx)
    mods = _modulation(cvec, w_mod, b_mod)[:, :nb + 1].reshape(depth, nb + 1, NMOD, D)
    cos_e, sin_e = _rope_tables(nb, s, cl)

    for i in range(depth):
        kind = i % 3
        j = i // 3
        ctx_out = i < depth - 1
        mod = mods[i]
        g_pre = pre_norm_mix[i].reshape(1, D)
        g_post = post_norm_mix[i].reshape(1, D)

        if kind == 0:
            mixed, mixed_ctx = _fourier_mix(x_all, g_pre, mod, nb, s, cl, ctx_out)
            w_out = fnet_w_out[j].astype(BF16)
        else:
            if kind == 1:
                w_qkv, w_out = gqa_w_qkv[j], gqa_w_o[j].astype(BF16)
                qg, kg = gqa_q_norm[j], gqa_k_norm[j]
            else:
                w_qkv, w_out = swa_w_qkv[j], swa_w_o[j].astype(BF16)
                qg = kg = jnp.ones((HD,), F32)
            head_gain = jnp.concatenate([jnp.tile(qg * (ATTN_SCALE * LOG2E), NH), jnp.tile(kg, NKV), jnp.ones((NKV * HD,), F32)])
            head_gain = head_gain.reshape(1, QKV)
            qkv = _qkv_proj(x_all, g_pre, mod, w_qkv.astype(BF16), head_gain, cos_e, sin_e,
                            s // TM_QKV, nb, do_norm=(kind == 1))
            if kind == 1:
                mixed = _global_attention(qkv, nb, s, cl)
                sink = jnp.zeros((NH,), F32)
            else:
                sink = swa_sink[j]
                mixed = _window_attention(qkv, sink, nb, s, cl)
            mixed_ctx = _ctx_attention(qkv, sink, nb, s, cl, use_sink=(kind == 2)) if ctx_out else None

        rows = r_all if ctx_out else r_lat
        x_all = _oproj(mixed, mixed_ctx if ctx_out else mixed, w_out, x_all, g_post, mod,
                       s // TM_OPROJ, nb, ctx_out)

        wr = jnp.zeros((D, LANES), F32).at[:, :NG].set(router_group_w[i]).at[:, LANE_E0:LANE_E0 + NE].set(router_expert_w[i])
        br = jnp.zeros((1, LANES), F32).at[0, :NG].set(router_group_b[i]).at[0, LANE_E0:LANE_E0 + NE].set(router_expert_b[i])
        x_all = _moe(x_all, i, pre_norm_ffn[i].reshape(1, D), post_norm_ffn[i].reshape(1, D), mod, wr, br,
                     expert_w_gate, expert_w_up, expert_w_down, s // TM_MOE, nb, rows,
                     r_all if ctx_out else r_lat)
    return x_all.reshape(nb, s, D)
```

```python
import functools
import math

import numpy as np
import jax
import jax.numpy as jnp
from jax import lax
from jax.experimental import pallas as pl
from jax.experimental.pallas import tpu as pltpu

F32 = jnp.float32
BF16 = jnp.bfloat16
U32 = jnp.uint32

D = 2048
DP = D // 2
HD = 128
NH = 16
NKV = 4
GQ = NH // NKV
QKV = (NH + 2 * NKV) * HD
GRID_W = 64
ROPE_THETA = 10000.0
ROT_FREQS = HD // 4
WINDOW = 128
FG = 8
FC = D // FG
NG = 4
EPG = 8
NE = NG * EPG
FF = 512
NMOD = 6
EPS = 1e-6
ATTN_SCALE = HD ** -0.5
DEPTH = 4

LANES = 128
SUB = 8
VMEM_LIMIT = 56 * 1024 * 1024

TM_QKV = 512
TN_QKV = 256
TM_OPROJ = 256
TQ_FLASH = 1024
TK_FLASH = 1024
FLASH_MARGIN = 60.0
LOG2E = math.log2(math.e)
TQ_WIN = 512
TM_MOE = 256
TMX = 256
TD_F = 512
N2 = 128
NEG = -1e30


def _cp(sem, vmem=VMEM_LIMIT):
    return pltpu.CompilerParams(dimension_semantics=sem, vmem_limit_bytes=vmem)


def _dot(a, b):
    return jnp.dot(a, b, preferred_element_type=F32)


def _dot_nt(a, b):
    return lax.dot_general(a, b, (((1,), (1,)), ((), ())), preferred_element_type=F32)


def _rms(x):
    return x * lax.rsqrt(jnp.mean(x * x, axis=-1, keepdims=True) + EPS)


def _mod_kernel(c_ref, w_ref, b_ref, o_ref):
    a = c_ref[...]
    a = a * jax.nn.sigmoid(a)
    o_ref[...] = _dot(a.astype(BF16), w_ref[...].astype(BF16)) + b_ref[...]


def _modulation(cvec, w_mod, b_mod):
    depth, _, nm = w_mod.shape
    tn = 1024
    return pl.pallas_call(
        _mod_kernel,
        grid=(depth, nm // tn),
        in_specs=[
            pl.BlockSpec((SUB, D), lambda l, j: (0, 0)),
            pl.BlockSpec((None, D, tn), lambda l, j: (l, 0, j)),
            pl.BlockSpec((None, 1, tn), lambda l, j: (l, 0, j)),
        ],
        out_specs=pl.BlockSpec((None, SUB, tn), lambda l, j: (l, 0, j)),
        out_shape=jax.ShapeDtypeStruct((depth, SUB, nm), F32),
        compiler_params=_cp(("arbitrary", "arbitrary")),
        name="modulation",
    )(cvec, w_mod, b_mod.reshape(depth, 1, nm))


def _qkv_kernel(x_ref, g_ref, mod_ref, w_ref, hg_ref, cos_ref, sin_ref, o_ref, *, do_norm, n_rot):
    h = _rms(x_ref[...]) * g_ref[...]
    h = (h * (1.0 + mod_ref[1:2, :]) + mod_ref[0:1, :]).astype(BF16)
    cos = cos_ref[...]
    sin = sin_ref[...]
    lane = lax.broadcasted_iota(jnp.int32, cos.shape, 1)
    first = (lane % (2 * ROT_FREQS)) < ROT_FREQS
    for j in range(QKV // TN_QKV):
        cols = slice(j * TN_QKV, (j + 1) * TN_QKV)
        y = _dot(h, w_ref[:, cols])
        if j >= n_rot:
            o_ref[:, cols] = y.astype(BF16)
            continue
        outs = []
        for hh in range(TN_QKV // HD):
            yh = y[:, hh * HD:(hh + 1) * HD]
            if do_norm:
                yh = _rms(yh)
            yh = yh * hg_ref[:, j * TN_QKV + hh * HD:j * TN_QKV + (hh + 1) * HD]
            partner = jnp.where(first, pltpu.roll(yh, HD - ROT_FREQS, 1), pltpu.roll(yh, ROT_FREQS, 1))
            outs.append(yh * cos + partner * sin)
        o_ref[:, cols] = jnp.concatenate(outs, axis=1).astype(BF16)


def _qkv_proj(x_all, gain, mod, w_bf, head_gain, cos_e, sin_e, tpb, nb, do_norm):
    r = x_all.shape[0]
    n_rot = (NH + NKV) * HD // TN_QKV
    return pl.pallas_call(
        functools.partial(_qkv_kernel, do_norm=do_norm, n_rot=n_rot),
        grid=(r // TM_QKV,),
        in_specs=[
            pl.BlockSpec((TM_QKV, D), lambda i: (i, 0)),
            pl.BlockSpec((1, D), lambda i: (0, 0)),
            pl.BlockSpec((None, NMOD, D), lambda i: (jnp.minimum(i // tpb, nb), 0, 0)),
            pl.BlockSpec((D, QKV), lambda i: (0, 0)),
            pl.BlockSpec((1, QKV), lambda i: (0, 0)),
            pl.BlockSpec((TM_QKV, HD), lambda i: (i, 0)),
            pl.BlockSpec((TM_QKV, HD), lambda i: (i, 0)),
        ],
        out_specs=pl.BlockSpec((TM_QKV, QKV), lambda i: (i, 0)),
        out_shape=jax.ShapeDtypeStruct((r, QKV), BF16),
        compiler_params=_cp(("parallel",)),
        name="qkv_proj",
    )(x_all, gain, mod, w_bf, head_gain, cos_e, sin_e)


def _flash_kernel(q_ref, kc_ref, vc_ref, k_ref, v_ref, o_ref, m_scr, acc_scr):
    j = pl.program_id(3)

    def chunks(s):
        return [s[:, c * LANES:(c + 1) * LANES] for c in range(s.shape[1] // LANES)]

    def rebase_update(k, v1):
        for g in range(GQ):
            ch = chunks(_dot_nt(q_ref[:, g * HD:(g + 1) * HD], k))
            m_prev = m_scr[g]
            m_new = jnp.maximum(m_prev, jnp.max(functools.reduce(jnp.maximum, ch), axis=-1, keepdims=True))
            alpha = jnp.exp2(m_prev - m_new)
            p = jnp.concatenate([jnp.exp2(c - m_new).astype(BF16) for c in ch], axis=1)
            acc = acc_scr[g]
            acc_scr[g] = jnp.concatenate([acc[:, :HD] * alpha, acc[:, HD:] * alpha], axis=1) + _dot(p, v1)
            m_scr[g] = m_new

    @pl.when(j == 0)
    def _():
        m_scr[...] = jnp.full(m_scr.shape, NEG, F32)
        acc_scr[...] = jnp.zeros(acc_scr.shape, F32)
        vc = vc_ref[...]
        rebase_update(kc_ref[...], jnp.concatenate([vc, jnp.ones_like(vc)], axis=1))

    k = k_ref[...]
    v = v_ref[...]
    v1 = jnp.concatenate([v, jnp.ones_like(v)], axis=1)
    ps = []
    excess = None
    for g in range(GQ):
        ch = chunks(_dot_nt(q_ref[:, g * HD:(g + 1) * HD], k))
        m = m_scr[g]
        over = functools.reduce(jnp.maximum, ch) - m
        excess = over if excess is None else jnp.maximum(excess, over)
        ps.append(jnp.concatenate([jnp.exp2(c - m).astype(BF16) for c in ch], axis=1))
    worst = jnp.max(jnp.max(excess, axis=0, keepdims=True), axis=1, keepdims=True)[0, 0]

    @pl.when(worst <= FLASH_MARGIN)
    def _():
        for g in range(GQ):
            acc_scr[g] += _dot(ps[g], v1)

    @pl.when(worst > FLASH_MARGIN)
    def _():
        rebase_update(k, v1)

    @pl.when(j == pl.num_programs(3) - 1)
    def _():
        for g in range(GQ):
            acc = acc_scr[g]
            o_ref[:, g * HD:(g + 1) * HD] = (acc[:, :HD] / acc[:, HD:]).astype(BF16)


def _global_attention(qkv, nb, s, c):
    nq = s // TQ_FLASH
    nk = s // TK_FLASH
    kcol = NH
    vcol = NH + NKV
    cblk = nb * s // c
    return pl.pallas_call(
        _flash_kernel,
        grid=(nb, NKV, nq, nk),
        in_specs=[
            pl.BlockSpec((TQ_FLASH, GQ * HD), lambda b, h, i, j: (b * nq + i, h)),
            pl.BlockSpec((c, HD), lambda b, h, i, j: (cblk + b, kcol + h)),
            pl.BlockSpec((c, HD), lambda b, h, i, j: (cblk + b, vcol + h)),
            pl.BlockSpec((TK_FLASH, HD), lambda b, h, i, j: (b * nk + j, kcol + h)),
            pl.BlockSpec((TK_FLASH, HD), lambda b, h, i, j: (b * nk + j, vcol + h)),
        ],
        out_specs=pl.BlockSpec((TQ_FLASH, GQ * HD), lambda b, h, i, j: (b * nq + i, h)),
        out_shape=jax.ShapeDtypeStruct((nb * s, NH * HD), BF16),
        scratch_shapes=[
            pltpu.VMEM((GQ, TQ_FLASH, LANES), F32),
            pltpu.VMEM((GQ, TQ_FLASH, 2 * HD), F32),
        ],
        compiler_params=_cp(("parallel", "parallel", "parallel", "arbitrary")),
        name="global_attention",
    )(qkv, qkv, qkv, qkv, qkv)


def _window_kernel(sink_ref, q_ref, kc_ref, vc_ref, kp_ref, vp_ref, k0_ref, v0_ref, kn_ref, vn_ref, bias_ref, o_ref):
    h = pl.program_id(1)
    k = jnp.concatenate([kc_ref[...], kp_ref[...], k0_ref[...], kn_ref[...]], axis=0)
    v = jnp.concatenate([vc_ref[...], vp_ref[...], v0_ref[...], vn_ref[...]], axis=0)
    v1 = jnp.concatenate([v, jnp.ones_like(v)], axis=1)
    bias = bias_ref[...]
    for g in range(GQ):
        sink = sink_ref[h * GQ + g] * LOG2E
        s = _dot_nt(q_ref[:, g * HD:(g + 1) * HD], k) + bias
        m = jnp.maximum(jnp.max(s, axis=-1, keepdims=True), sink)
        acc = _dot(jnp.exp2(s - m).astype(BF16), v1)
        o_ref[:, g * HD:(g + 1) * HD] = (acc[:, :HD] / (acc[:, HD:] + jnp.exp2(sink - m))).astype(BF16)


def _window_bias(c):
    q = jnp.arange(TQ_WIN, dtype=jnp.int32)[:, None]
    kk = jnp.arange(WINDOW, dtype=jnp.int32)[None, :]
    before = kk >= q
    after = q >= TQ_WIN - WINDOW + kk
    inside = jnp.abs(q - jnp.arange(TQ_WIN, dtype=jnp.int32)[None, :]) <= WINDOW
    ctx = jnp.ones((TQ_WIN, c), bool)
    variants = []
    for var in range(4):
        ok = jnp.concatenate([ctx, before & ((var & 1) == 0), inside, after & ((var & 2) == 0)], axis=1)
        variants.append(jnp.where(ok, 0.0, NEG).astype(F32))
    return jnp.stack(variants)


def _window_attention(qkv, sink, nb, s, c):
    tq = TQ_WIN
    nq = s // tq
    per = tq // WINDOW
    nw = s // WINDOW
    kcol = NH
    vcol = NH + NKV
    cblk = nb * s // c

    def edge(col, off):
        return pl.BlockSpec((WINDOW, HD), lambda b, h, i: (b * nw + jnp.clip(i * per + off, 0, nw - 1), col + h))

    def tile(col):
        return pl.BlockSpec((tq, HD), lambda b, h, i: (b * nq + i, col + h))

    return pl.pallas_call(
        _window_kernel,
        grid=(nb, NKV, nq),
        in_specs=[
            pl.BlockSpec(memory_space=pltpu.SMEM),
            pl.BlockSpec((tq, GQ * HD), lambda b, h, i: (b * nq + i, h)),
            pl.BlockSpec((c, HD), lambda b, h, i: (cblk + b, kcol + h)),
            pl.BlockSpec((c, HD), lambda b, h, i: (cblk + b, vcol + h)),
            edge(kcol, -1), edge(vcol, -1), tile(kcol), tile(vcol), edge(kcol, per), edge(vcol, per),
            pl.BlockSpec((None, tq, c + 2 * WINDOW + tq),
                         lambda b, h, i: ((i == 0).astype(jnp.int32) + 2 * (i == nq - 1).astype(jnp.int32), 0, 0)),
        ],
        out_specs=pl.BlockSpec((tq, GQ * HD), lambda b, h, i: (b * nq + i, h)),
        out_shape=jax.ShapeDtypeStruct((nb * s, NH * HD), BF16),
        compiler_params=_cp(("parallel", "parallel", "parallel")),
        name="window_attention",
    )(sink, qkv, qkv, qkv, qkv, qkv, qkv, qkv, qkv, qkv, _window_bias(c))


def _ctx_attn_kernel(sink_ref, q_ref, k_ref, v_ref, o_ref, *, use_sink):
    h = pl.program_id(1)
    k = k_ref[...]
    v = v_ref[...]
    for g in range(GQ):
        s = _dot_nt(q_ref[:, g * HD:(g + 1) * HD], k)
        m = jnp.max(s, axis=-1, keepdims=True)
        if use_sink:
            sink = sink_ref[h * GQ + g] * LOG2E
            m = jnp.maximum(m, sink)
        p = jnp.exp2(s - m)
        l = jnp.sum(p, axis=-1, keepdims=True)
        if use_sink:
            l = l + jnp.exp2(sink - m)
        o_ref[:, g * HD:(g + 1) * HD] = (_dot(p.astype(BF16), v) / l).astype(BF16)


def _ctx_attention(qkv, sink, nb, s, c, use_sink):
    cblk = nb * s // c
    return pl.pallas_call(
        functools.partial(_ctx_attn_kernel, use_sink=use_sink),
        grid=(nb, NKV),
        in_specs=[
            pl.BlockSpec(memory_space=pltpu.SMEM),
            pl.BlockSpec((c, GQ * HD), lambda b, h: (cblk + b, h)),
            pl.BlockSpec((c, HD), lambda b, h: (cblk + b, NH + h)),
            pl.BlockSpec((c, HD), lambda b, h: (cblk + b, NH + NKV + h)),
        ],
        out_specs=pl.BlockSpec((c, GQ * HD), lambda b, h: (b, h)),
        out_shape=jax.ShapeDtypeStruct((nb * c, NH * HD), BF16),
        compiler_params=_cp(("arbitrary", "arbitrary")),
        name="ctx_attention",
    )(sink, qkv, qkv, qkv)


def _oproj_kernel(a_ref, ac_ref, w_ref, x_ref, g_ref, mod_ref, o_ref, *, nlat):
    def project(src_ref):
        y = _dot(src_ref[...].astype(BF16), w_ref[...])
        o_ref[...] = x_ref[...] + mod_ref[2:3, :] * (_rms(y) * g_ref[...])

    i = pl.program_id(0)

    @pl.when(i < nlat)
    def _():
        project(a_ref)

    @pl.when(i >= nlat)
    def _():
        project(ac_ref)


def _oproj(a_lat, a_ctx, w_bf, x_all, gain, mod, tpb, nb, with_ctx):
    tm = TM_OPROJ
    nlat = a_lat.shape[0] // tm
    ntiles = nlat + (a_ctx.shape[0] // tm if with_ctx else 0)
    return pl.pallas_call(
        functools.partial(_oproj_kernel, nlat=nlat),
        grid=(ntiles,),
        in_specs=[
            pl.BlockSpec((tm, D), lambda i: (jnp.minimum(i, nlat - 1), 0)),
            pl.BlockSpec((tm, D), lambda i: (jnp.maximum(i - nlat, 0), 0)),
            pl.BlockSpec((D, D), lambda i: (0, 0)),
            pl.BlockSpec((tm, D), lambda i: (i, 0)),
            pl.BlockSpec((1, D), lambda i: (0, 0)),
            pl.BlockSpec((None, NMOD, D), lambda i: (jnp.minimum(i // tpb, nb), 0, 0)),
        ],
        out_specs=pl.BlockSpec((tm, D), lambda i: (i, 0)),
        out_shape=jax.ShapeDtypeStruct(x_all.shape, F32),
        input_output_aliases={3: 0},
        compiler_params=_cp(("parallel",)),
        name="oproj",
    )(a_lat, a_ctx, w_bf, x_all, gain, mod)


def _dft_mats(n):
    k = np.arange(n)
    ang = 2.0 * np.pi * ((k[:, None] * k[None, :]) % n) / n
    return np.cos(ang) / math.sqrt(n), np.sin(ang) / math.sqrt(n)


def _stage1_matrix(n1):
    c, s = _dft_mats(n1)
    eye = np.eye(SUB)
    return np.concatenate([np.kron(c, eye), -np.kron(s, eye)], axis=0)


def _stage2_matrix():
    c, s = _dft_mats(N2)
    return np.block([[c, s], [-s, c]])


def _channel_matrix():
    c, s = _dft_mats(FC)
    return np.concatenate([c, s], axis=0)


def _fft1_kernel(x_ref, g_ref, mod_ref, kr_ref, tc_ref, ts_ref, ar_ref, ai_ref, rstd_scr):
    dt = pl.program_id(2)
    n1 = x_ref.shape[0]
    rows = n1 * SUB

    @pl.when(dt == 0)
    def _():
        x = x_ref[...].reshape(rows, D)
        rstd_scr[...] = lax.rsqrt(jnp.mean(x * x, axis=-1, keepdims=True) + EPS)

    col = pl.multiple_of(dt * TD_F, TD_F)
    x = x_ref[:, :, pl.ds(col, TD_F)].reshape(rows, TD_F)
    h = x * rstd_scr[...] * g_ref[...]
    h = h * (1.0 + mod_ref[1:2, :]) + mod_ref[0:1, :]
    y = _dot(kr_ref[...], h.astype(BF16))
    yr = y[:rows]
    yi = y[rows:]
    tc = tc_ref[...]
    ts = ts_ref[...]
    for cc in range(TD_F // LANES):
        sl = slice(cc * LANES, (cc + 1) * LANES)
        a = yr[:, sl]
        b = yi[:, sl]
        ar_ref[:, :, sl] = (a * tc + b * ts).reshape(n1, SUB, LANES)
        ai_ref[:, :, sl] = (b * tc - a * ts).reshape(n1, SUB, LANES)


def _fft2_kernel(ar_ref, ai_ref, w2_ref, cs_ref, o_ref):
    yr, yi = [], []
    for jj in range(SUB):
        rows = slice(jj * N2, (jj + 1) * N2)
        a = jnp.concatenate([ar_ref[rows, :], ai_ref[rows, :]], axis=0).astype(BF16)
        y = _dot(w2_ref[...], a)
        yr.append(y[:N2])
        yi.append(y[N2:])
    yr = jnp.concatenate(yr, axis=0).astype(BF16)
    yi = jnp.concatenate(yi, axis=0).astype(BF16)
    for gi in range(TD_F // FC):
        sl = slice(gi * FC, (gi + 1) * FC)
        res = _dot(jnp.concatenate([yr[:, sl], yi[:, sl]], axis=1), cs_ref[...])
        for jj in range(SUB):
            o_ref[:, jj, sl] = res[jj * N2:(jj + 1) * N2]


def _fnet_ctx_kernel(x_ref, g_ref, mod_ref, f_ref, cs_ref, o_ref):
    c = x_ref.shape[0]
    h = _rms(x_ref[...]) * g_ref[...]
    h = h * (1.0 + mod_ref[1:2, :]) + mod_ref[0:1, :]
    y = _dot(f_ref[...], h.astype(BF16))
    yr = y[:c].astype(BF16)
    yi = y[c:].astype(BF16)
    for gi in range(FG):
        sl = slice(gi * FC, (gi + 1) * FC)
        z = jnp.concatenate([yr[:, sl], yi[:, sl]], axis=1)
        o_ref[:, sl] = _dot(z, cs_ref[...])


def _fourier_mix(x_all, gain, mod, nb, s, c, with_ctx):
    r = x_all.shape[0]
    n1 = s // N2
    nch = N2 // SUB
    ndt = D // TD_F
    kr = jnp.asarray(_stage1_matrix(n1), BF16)
    k2 = jnp.asarray(_stage2_matrix(), BF16)
    cs = jnp.asarray(_channel_matrix(), BF16)
    rows = n1 * SUB
    k1 = jnp.arange(n1, dtype=jnp.int32)[None, :, None]
    n2 = (jnp.arange(nch, dtype=jnp.int32)[:, None, None] * SUB + jnp.arange(SUB, dtype=jnp.int32)[None, None, :])
    ang = (2.0 * np.pi / s) * ((k1 * n2) % s).astype(F32).reshape(nch, rows, 1)
    tc = jnp.broadcast_to(jnp.cos(ang), (nch, rows, LANES))
    ts = jnp.broadcast_to(jnp.sin(ang), (nch, rows, LANES))

    x4 = x_all.reshape(r // N2, N2, D)
    ar, ai = pl.pallas_call(
        _fft1_kernel,
        grid=(nb, nch, ndt),
        in_specs=[
            pl.BlockSpec((n1, SUB, D), lambda b, ch, dt: (b, ch, 0)),
            pl.BlockSpec((1, TD_F), lambda b, ch, dt: (0, dt)),
            pl.BlockSpec((None, NMOD, TD_F), lambda b, ch, dt: (b, 0, dt)),
            pl.BlockSpec((2 * rows, rows), lambda b, ch, dt: (0, 0)),
            pl.BlockSpec((None, rows, LANES), lambda b, ch, dt: (ch, 0, 0)),
            pl.BlockSpec((None, rows, LANES), lambda b, ch, dt: (ch, 0, 0)),
        ],
        out_specs=[
            pl.BlockSpec((n1, SUB, TD_F), lambda b, ch, dt: (b, ch, dt)),
            pl.BlockSpec((n1, SUB, TD_F), lambda b, ch, dt: (b, ch, dt)),
        ],
        out_shape=[jax.ShapeDtypeStruct((nb * n1, N2, D), F32)] * 2,
        scratch_shapes=[pltpu.VMEM((rows, 1), F32)],
        compiler_params=_cp(("parallel", "parallel", "arbitrary")),
        name="fnet_stage1",
    )(x4, gain, mod, kr, tc, ts)

    ar2 = ar.reshape(nb * s, D)
    ai2 = ai.reshape(nb * s, D)
    nkc = n1 // SUB
    mixed = pl.pallas_call(
        _fft2_kernel,
        grid=(nb, nkc, ndt),
        in_specs=[
            pl.BlockSpec((N2 * SUB, TD_F), lambda b, kc, dt: (b * nkc + kc, dt)),
            pl.BlockSpec((N2 * SUB, TD_F), lambda b, kc, dt: (b * nkc + kc, dt)),
            pl.BlockSpec((2 * N2, 2 * N2), lambda b, kc, dt: (0, 0)),
            pl.BlockSpec((2 * FC, FC), lambda b, kc, dt: (0, 0)),
        ],
        out_specs=pl.BlockSpec((N2, SUB, TD_F), lambda b, kc, dt: (b, kc, dt)),
        out_shape=jax.ShapeDtypeStruct((nb * s // n1, n1, D), F32),
        compiler_params=_cp(("parallel", "parallel", "parallel")),
        name="fnet_stage2",
    )(ar2, ai2, k2, cs)
    mixed = mixed.reshape(nb * s, D)

    mixed_ctx = None
    if with_ctx:
        fc, fs = _dft_mats(c)
        fmat = jnp.asarray(np.concatenate([fc, -fs], axis=0), BF16)
        cblk = nb * s // c
        mixed_ctx = pl.pallas_call(
            _fnet_ctx_kernel,
            grid=(nb,),
            in_specs=[
                pl.BlockSpec((c, D), lambda b: (cblk + b, 0)),
                pl.BlockSpec((1, D), lambda b: (0, 0)),
                pl.BlockSpec((None, NMOD, D), lambda b: (nb, 0, 0)),
                pl.BlockSpec((2 * c, c), lambda b: (0, 0)),
                pl.BlockSpec((2 * FC, FC), lambda b: (0, 0)),
            ],
            out_specs=pl.BlockSpec((c, D), lambda b: (b, 0)),
            out_shape=jax.ShapeDtypeStruct((nb * c, D), F32),
            compiler_params=_cp(("arbitrary",)),
            name="fnet_ctx",
        )(x_all, gain, mod, fmat, cs)
    return mixed, mixed_ctx


M_E1, M_E2, M_R1, M_R2, M_W1, M_W2, M_PG = range(7)
LANE_E0 = NG


def _ffn_tokens(x_ref, g_ref, mod_ref):
    tok = _rms(x_ref[...]) * g_ref[...]
    return tok * (1.0 + mod_ref[4:5, :]) + mod_ref[3:4, :]


def _route_kernel(x_ref, g_ref, mod_ref, wr_ref, br_ref, meta_ref, cnt_ref, base_scr):
    i = pl.program_id(0)
    tm = x_ref.shape[0]

    @pl.when(i == 0)
    def _():
        base_scr[...] = jnp.zeros(base_scr.shape, F32)

    tok = _ffn_tokens(x_ref, g_ref, mod_ref)
    wr = wr_ref[...]
    w_hi = wr.astype(BF16)
    w_lo = (wr - w_hi.astype(F32)).astype(BF16)
    t_hi = tok.astype(BF16)
    t_lo = (tok - t_hi.astype(F32)).astype(BF16)
    logits = _dot(t_hi, w_hi) + (_dot(t_hi, w_lo) + _dot(t_lo, w_hi)) + br_ref[...]

    lane = lax.broadcasted_iota(jnp.int32, logits.shape, 1)
    big = jnp.int32(LANES)
    gmask = lane < NG
    gl = jnp.where(gmask, logits, NEG)
    gmax = jnp.max(gl, axis=-1, keepdims=True)
    gsum = jnp.sum(jnp.where(gmask, jnp.exp(gl - gmax), 0.0), axis=-1, keepdims=True)
    gidx = jnp.min(jnp.where(gl == gmax, lane, big), axis=-1, keepdims=True)
    lo = LANE_E0 + gidx * EPG
    emask = (lane >= lo) & (lane < lo + EPG)
    el = jnp.where(emask, logits, NEG)
    emax = jnp.max(el, axis=-1, keepdims=True)
    idx1 = jnp.min(jnp.where(el == emax, lane, big), axis=-1, keepdims=True)
    el2 = jnp.where(lane == idx1, NEG, el)
    e2max = jnp.max(el2, axis=-1, keepdims=True)
    idx2 = jnp.min(jnp.where(el2 == e2max, lane, big), axis=-1, keepdims=True)
    t = jnp.exp(e2max - emax)
    w1 = 1.0 / (1.0 + t)
    w2 = t / (1.0 + t)

    oh1 = (lane == idx1).astype(F32)
    oh2 = (lane == idx2).astype(F32)
    rr = lax.broadcasted_iota(jnp.int32, (tm, tm), 0)
    cc = lax.broadcasted_iota(jnp.int32, (tm, tm), 1)
    tri = (cc < rr).astype(BF16)
    base = base_scr[0:1, :]
    cnt1 = jnp.sum(oh1, axis=0, keepdims=True)
    cnt2 = jnp.sum(oh2, axis=0, keepdims=True)
    pre1 = _dot(tri, oh1.astype(BF16)) + base
    pre2 = _dot(tri, oh2.astype(BF16)) + base + cnt1
    r1 = jnp.sum(oh1 * pre1, axis=-1, keepdims=True)
    r2 = jnp.sum(oh2 * pre2, axis=-1, keepdims=True)
    new_base = base + cnt1 + cnt2
    base_scr[0:1, :] = new_base
    cnt_ref[...] = jnp.broadcast_to(new_base, cnt_ref.shape)

    rec = jnp.zeros(logits.shape, F32)
    for k, val in ((M_E1, (idx1 - LANE_E0).astype(F32)), (M_E2, (idx2 - LANE_E0).astype(F32)),
                   (M_R1, r1), (M_R2, r2), (M_W1, w1), (M_W2, w2), (M_PG, 1.0 / gsum)):
        rec = jnp.where(lane == k, val, rec)
    meta_ref[...] = rec


def _pack_pairs(t):
    half = t.shape[1] // 2
    hi = lax.bitcast_convert_type(t[:, :half].astype(BF16).astype(F32), U32)
    lo = lax.bitcast_convert_type(t[:, half:].astype(BF16).astype(F32), U32)
    return hi | (lo >> 16)


def _unpack_pairs(w):
    hi = lax.bitcast_convert_type(w & jnp.uint32(0xFFFF0000), F32)
    lo = lax.bitcast_convert_type(w << 16, F32)
    return jnp.concatenate([hi, lo], axis=1)


def _row_copy_wait(hbm_ref, vmem_ref, sem):
    pltpu.make_async_copy(hbm_ref.at[pl.ds(0, vmem_ref.shape[0])], vmem_ref, sem).wait()


def _dispatch_kernel(pt_ref, d_ref, x_ref, g_ref, mod_ref, xs_hbm, buf, zbuf, sem, zsem):
    i = pl.program_id(0)
    nt = pl.num_programs(0)
    tm = x_ref.shape[0]
    slot = i % 2

    @pl.when(i == 0)
    def _():
        zbuf[...] = jnp.zeros(zbuf.shape, U32)

        def zero_tile(t, carry):
            @pl.when(pt_ref[t] != 0)
            def _():
                pltpu.make_async_copy(zbuf, xs_hbm.at[pl.ds(pl.multiple_of(t * TMX, TMX), TMX)], zsem).start()
            return carry

        def zero_wait(t, carry):
            @pl.when(pt_ref[t] != 0)
            def _():
                _row_copy_wait(xs_hbm, zbuf, zsem)
            return carry

        lax.fori_loop(0, pt_ref.shape[0], zero_tile, 0)
        lax.fori_loop(0, pt_ref.shape[0], zero_wait, 0)

    @pl.when(i >= 2)
    def _():
        _row_copy_wait(xs_hbm, buf.at[slot], sem.at[slot])
        _row_copy_wait(xs_hbm, buf.at[slot], sem.at[slot])

    buf[slot] = _pack_pairs(_ffn_tokens(x_ref, g_ref, mod_ref))

    def issue(rw, carry):
        src = buf.at[slot, pl.ds(rw, 1)]
        pltpu.make_async_copy(src, xs_hbm.at[pl.ds(d_ref[0, rw], 1)], sem.at[slot]).start()
        pltpu.make_async_copy(src, xs_hbm.at[pl.ds(d_ref[0, tm + rw], 1)], sem.at[slot]).start()
        return carry

    lax.fori_loop(0, tm, issue, 0, unroll=8)

    @pl.when(i == nt - 1)
    def _():
        _row_copy_wait(xs_hbm, buf.at[slot], sem.at[slot])
        _row_copy_wait(xs_hbm, buf.at[slot], sem.at[slot])

        @pl.when(nt >= 2)
        def _():
            _row_copy_wait(xs_hbm, buf.at[1 - slot], sem.at[1 - slot])
            _row_copy_wait(xs_hbm, buf.at[1 - slot], sem.at[1 - slot])


def _expert_kernel(te_ref, nv_ref, xs_ref, wg_ref, wu_ref, wd_ref, o_ref, wg_s, wu_s, wd_s):
    t = pl.program_id(0)

    @pl.when(t < nv_ref[0])
    def _():
        prev = te_ref[jnp.maximum(t - 1, 0)]

        @pl.when((t == 0) | (te_ref[t] != prev))
        def _():
            wg_s[...] = wg_ref[...].astype(BF16)
            wu_s[...] = wu_ref[...].astype(BF16)
            wd_s[...] = wd_ref[...].astype(BF16)

        x = _unpack_pairs(xs_ref[...]).astype(BF16)
        a = _dot(x, wg_s[...])
        u = _dot(x, wu_s[...])
        hmid = (a * jax.nn.sigmoid(a) * u).astype(BF16)
        o_ref[...] = _pack_pairs(_dot(hmid, wd_s[...]))

    @pl.when(t >= nv_ref[0])
    def _():
        o_ref[...] = jnp.zeros(o_ref.shape, U32)


def _combine_kernel(d_ref, dn_ref, meta_ref, ys_hbm, x_ref, g_ref, mod_ref, o_ref, buf, sem):
    i = pl.program_id(0)
    nt = pl.num_programs(0)
    tm = x_ref.shape[0]
    slot = i % 2

    def gather(idx_ref, sl):
        def issue(rw, carry):
            pltpu.make_async_copy(ys_hbm.at[pl.ds(idx_ref[0, rw], 1)], buf.at[sl, 0, pl.ds(rw, 1)], sem.at[sl]).start()
            pltpu.make_async_copy(ys_hbm.at[pl.ds(idx_ref[0, tm + rw], 1)], buf.at[sl, 1, pl.ds(rw, 1)], sem.at[sl]).start()
            return carry

        lax.fori_loop(0, tm, issue, 0, unroll=8)

    @pl.when(i == 0)
    def _():
        gather(d_ref, 0)

    @pl.when(i + 1 < nt)
    def _():
        gather(dn_ref, 1 - slot)

    _row_copy_wait(ys_hbm, buf.at[slot, 0], sem.at[slot])
    _row_copy_wait(ys_hbm, buf.at[slot, 1], sem.at[slot])
    meta = meta_ref[...]
    w1 = meta[:, M_W1:M_W1 + 1]
    w2 = meta[:, M_W2:M_W2 + 1]
    pg = meta[:, M_PG:M_PG + 1]
    y = w1 * _unpack_pairs(buf[slot, 0]) + w2 * _unpack_pairs(buf[slot, 1])
    o_ref[...] = x_ref[...] + mod_ref[5:6, :] * (pg * (_rms(y) * g_ref[...]))


def _moe(x_all, layer, pre_gain, post_gain, mod, wr, br, w_gate, w_up, w_down, tpb, nb, nrows, out_rows):
    tm = TM_MOE
    nt = nrows // tm
    mod_map = lambda i: (jnp.minimum(i // tpb, nb), 0, 0)
    meta, cnt = pl.pallas_call(
        _route_kernel,
        grid=(nt,),
        in_specs=[
            pl.BlockSpec((tm, D), lambda i: (i, 0)),
            pl.BlockSpec((1, D), lambda i: (0, 0)),
            pl.BlockSpec((None, NMOD, D), mod_map),
            pl.BlockSpec((D, LANES), lambda i: (0, 0)),
            pl.BlockSpec((1, LANES), lambda i: (0, 0)),
        ],
        out_specs=[
            pl.BlockSpec((tm, LANES), lambda i: (i, 0)),
            pl.BlockSpec((SUB, LANES), lambda i: (0, 0)),
        ],
        out_shape=[
            jax.ShapeDtypeStruct((nrows, LANES), F32),
            jax.ShapeDtypeStruct((SUB, LANES), F32),
        ],
        scratch_shapes=[pltpu.VMEM((SUB, LANES), F32)],
        compiler_params=_cp(("arbitrary",)),
        name="moe_route",
    )(x_all, pre_gain, mod, wr, br)

    ntile = nrows * 2 // TMX + NE
    nslot = ntile * TMX
    e = meta[:, M_E1:M_E2 + 1].astype(jnp.int32)
    rk = meta[:, M_R1:M_R2 + 1].astype(jnp.int32)
    counts = cnt[0, LANE_E0:LANE_E0 + NE].astype(jnp.int32)
    padded = (counts + TMX - 1) // TMX * TMX
    seg_end = jnp.cumsum(padded)
    seg_start = seg_end - padded
    dest = seg_start[e] + rk
    dtile = dest.reshape(nt, tm, 2).transpose(0, 2, 1).reshape(nt, 1, 2 * tm)
    nvalid = (seg_end[-1] // TMX).astype(jnp.int32).reshape(1)
    tile_id = jnp.arange(ntile, dtype=jnp.int32)
    te_raw = jnp.sum((tile_id[:, None] * TMX >= seg_end[None, :]).astype(jnp.int32), axis=1)
    te = jnp.minimum(te_raw, te_raw[jnp.maximum(nvalid[0] - 1, 0)])
    last_of_seg = jnp.concatenate([te_raw[1:] != te_raw[:-1], jnp.ones((1,), bool)])
    partial = (last_of_seg | (tile_id >= nvalid[0])).astype(jnp.int32)

    xs = pl.pallas_call(
        _dispatch_kernel,
        grid_spec=pltpu.PrefetchScalarGridSpec(
            num_scalar_prefetch=1,
            grid=(nt,),
            in_specs=[
                pl.BlockSpec((None, 1, 2 * tm), lambda i, pt: (i, 0, 0), memory_space=pltpu.SMEM),
                pl.BlockSpec((tm, D), lambda i, pt: (i, 0)),
                pl.BlockSpec((1, D), lambda i, pt: (0, 0)),
                pl.BlockSpec((None, NMOD, D), lambda i, pt: mod_map(i)),
            ],
            out_specs=pl.BlockSpec(memory_space=pl.ANY),
            scratch_shapes=[
                pltpu.VMEM((2, tm, DP), U32),
                pltpu.VMEM((TMX, DP), U32),
                pltpu.SemaphoreType.DMA((2,)),
                pltpu.SemaphoreType.DMA(()),
            ],
        ),
        out_shape=jax.ShapeDtypeStruct((nslot, DP), U32),
        compiler_params=_cp(("arbitrary",)),
        name="moe_dispatch",
    )(partial, dtile, x_all, pre_gain, mod)

    ys = pl.pallas_call(
        _expert_kernel,
        grid_spec=pltpu.PrefetchScalarGridSpec(
            num_scalar_prefetch=2,
            grid=(ntile,),
            in_specs=[
                pl.BlockSpec((TMX, DP), lambda t, te_r, nv_r: (jnp.minimum(t, nv_r[0] - 1), 0)),
                pl.BlockSpec((None, None, D, FF), lambda t, te_r, nv_r: (layer, te_r[t], 0, 0)),
                pl.BlockSpec((None, None, D, FF), lambda t, te_r, nv_r: (layer, te_r[t], 0, 0)),
                pl.BlockSpec((None, None, FF, D), lambda t, te_r, nv_r: (layer, te_r[t], 0, 0)),
            ],
            out_specs=pl.BlockSpec((TMX, DP), lambda t, te_r, nv_r: (t, 0)),
            scratch_shapes=[
                pltpu.VMEM((D, FF), BF16),
                pltpu.VMEM((D, FF), BF16),
                pltpu.VMEM((FF, D), BF16),
            ],
        ),
        out_shape=jax.ShapeDtypeStruct((nslot, DP), U32),
        compiler_params=_cp(("arbitrary",)),
        name="moe_experts",
    )(te, nvalid, xs, w_gate, w_up, w_down)

    alias = {} if out_rows != x_all.shape[0] else {4: 0}
    return pl.pallas_call(
        _combine_kernel,
        grid=(nt,),
        in_specs=[
            pl.BlockSpec((None, 1, 2 * tm), lambda i: (i, 0, 0), memory_space=pltpu.SMEM),
            pl.BlockSpec((None, 1, 2 * tm), lambda i: (jnp.minimum(i + 1, nt - 1), 0, 0), memory_space=pltpu.SMEM),
            pl.BlockSpec((tm, LANES), lambda i: (i, 0)),
            pl.BlockSpec(memory_space=pl.ANY),
            pl.BlockSpec((tm, D), lambda i: (i, 0)),
            pl.BlockSpec((1, D), lambda i: (0, 0)),
            pl.BlockSpec((None, NMOD, D), mod_map),
        ],
        out_specs=pl.BlockSpec((tm, D), lambda i: (i, 0)),
        out_shape=jax.ShapeDtypeStruct((out_rows, D), F32),
        scratch_shapes=[pltpu.VMEM((2, 2, tm, DP), U32), pltpu.SemaphoreType.DMA((2,))],
        input_output_aliases=alias,
        compiler_params=_cp(("arbitrary",)),
        name="moe_combine",
    )(dtile, dtile, meta, ys, x_all, post_gain, mod)


def _rope_tables(nb, s, c):
    pos = jnp.arange(s, dtype=jnp.int32)
    row = (pos // GRID_W).astype(F32)
    colp = (pos % GRID_W).astype(F32)
    inv_freq = ROPE_THETA ** (-jnp.arange(ROT_FREQS, dtype=F32) / ROT_FREQS)
    ar = row[:, None] * inv_freq
    ac = colp[:, None] * inv_freq
    cos = jnp.concatenate([jnp.cos(ar), jnp.cos(ar), jnp.cos(ac), jnp.cos(ac)], axis=1)
    sin = jnp.concatenate([-jnp.sin(ar), jnp.sin(ar), -jnp.sin(ac), jnp.sin(ac)], axis=1)
    cos = jnp.concatenate([jnp.tile(cos, (nb, 1)), jnp.ones((nb * c, HD), F32)], axis=0)
    sin = jnp.concatenate([jnp.tile(sin, (nb, 1)), jnp.zeros((nb * c, HD), F32)], axis=0)
    return cos, sin


def kernel(x, c, ctx, c_ctx, pre_norm_mix, post_norm_mix, pre_norm_ffn, post_norm_ffn, w_mod, b_mod, fnet_w_out, gqa_w_qkv, gqa_q_norm, gqa_k_norm, gqa_w_o, swa_w_qkv, swa_sink, swa_w_o, router_group_w, router_group_b, router_expert_w, router_expert_b, expert_w_gate, expert_w_up, expert_w_down):
    nb, s, _ = x.shape
    cl = ctx.shape[1]
    depth = w_mod.shape[0]
    r_lat = nb * s
    r_all = r_lat + nb * cl
    assert s % TQ_FLASH == 0 and s % (N2 * SUB) == 0 and (nb * cl) % TM_QKV == 0 and r_all % (s // N2) == 0

    x_all = jnp.concatenate([x.reshape(r_lat, D), ctx.reshape(nb * cl, D)], axis=0)
    cvec = jnp.zeros((SUB, D), F32).at[:nb].set(c).at[nb].set(c_ctx)
    mods = _modulation(cvec, w_mod, b_mod)[:, :nb + 1].reshape(depth, nb + 1, NMOD, D)
    cos_e, sin_e = _rope_tables(nb, s, cl)

    for i in range(depth):
        kind = i % 3
        j = i // 3
        ctx_out = i < depth - 1
        mod = mods[i]
        g_pre = pre_norm_mix[i].reshape(1, D)
        g_post = post_norm_mix[i].reshape(1, D)

        if kind == 0:
            mixed, mixed_ctx = _fourier_mix(x_all, g_pre, mod, nb, s, cl, ctx_out)
            w_out = fnet_w_out[j].astype(BF16)
        else:
            if kind == 1:
                w_qkv, w_out = gqa_w_qkv[j], gqa_w_o[j].astype(BF16)
                qg, kg = gqa_q_norm[j], gqa_k_norm[j]
            else:
                w_qkv, w_out = swa_w_qkv[j], swa_w_o[j].astype(BF16)
                qg = kg = jnp.ones((HD,), F32)
            head_gain = jnp.concatenate([jnp.tile(qg * (ATTN_SCALE * LOG2E), NH), jnp.tile(kg, NKV), jnp.ones((NKV * HD,), F32)])
            head_gain = head_gain.reshape(1, QKV)
            qkv = _qkv_proj(x_all, g_pre, mod, w_qkv.astype(BF16), head_gain, cos_e, sin_e,
                            s // TM_QKV, nb, do_norm=(kind == 1))
            if kind == 1:
                mixed = _global_attention(qkv, nb, s, cl)
                sink = jnp.zeros((NH,), F32)
            else:
                sink = swa_sink[j]
                mixed = _window_attention(qkv, sink, nb, s, cl)
            mixed_ctx = _ctx_attention(qkv, sink, nb, s, cl, use_sink=(kind == 2)) if ctx_out else None

        rows = r_all if ctx_out else r_lat
        x_all = _oproj(mixed, mixed_ctx if ctx_out else mixed, w_out, x_all, g_post, mod,
                       s // TM_OPROJ, nb, ctx_out)

        wr = jnp.zeros((D, LANES), F32).at[:, :NG].set(router_group_w[i]).at[:, LANE_E0:LANE_E0 + NE].set(router_expert_w[i])
        br = jnp.zeros((1, LANES), F32).at[0, :NG].set(router_group_b[i]).at[0, LANE_E0:LANE_E0 + NE].set(router_expert_b[i])
        x_all = _moe(x_all, i, pre_norm_ffn[i].reshape(1, D), post_norm_ffn[i].reshape(1, D), mod, wr, br,
                     expert_w_gate, expert_w_up, expert_w_down, s // TM_MOE, nb, rows,
                     r_all if ctx_out else r_lat)
    return x_all.reshape(nb, s, D)
```

```python
import functools
import math

import numpy as np
import jax
import jax.numpy as jnp
from jax import lax
from jax.experimental import pallas as pl
from jax.experimental.pallas import tpu as pltpu

F32 = jnp.float32
BF16 = jnp.bfloat16
U32 = jnp.uint32

D = 2048
DP = D // 2
HD = 128
NH = 16
NKV = 4
GQ = NH // NKV
QKV = (NH + 2 * NKV) * HD
GRID_W = 64
ROPE_THETA = 10000.0
ROT_FREQS = HD // 4
WINDOW = 128
FG = 8
FC = D // FG
NG = 4
EPG = 8
NE = NG * EPG
FF = 512
NMOD = 6
EPS = 1e-6
ATTN_SCALE = HD ** -0.5
DEPTH = 4

LANES = 128
SUB = 8
VMEM_LIMIT = 56 * 1024 * 1024

TM_QKV = 512
TN_QKV = 256
TM_OPROJ = 512
TQ_FLASH = 1024
TK_FLASH = 1024
FLASH_MARGIN = 60.0
LOG2E = math.log2(math.e)
TQ_WIN = 512
TM_MOE = 256
TMX = 256
TD_F = 512
N2 = 128
NEG = -1e30


def _cp(sem, vmem=VMEM_LIMIT):
    return pltpu.CompilerParams(dimension_semantics=sem, vmem_limit_bytes=vmem)


def _dot(a, b):
    return jnp.dot(a, b, preferred_element_type=F32)


def _dot_nt(a, b):
    return lax.dot_general(a, b, (((1,), (1,)), ((), ())), preferred_element_type=F32)


def _rms(x):
    return x * lax.rsqrt(jnp.mean(x * x, axis=-1, keepdims=True) + EPS)


def _mod_kernel(c_ref, w_ref, b_ref, o_ref):
    a = c_ref[...]
    a = a * jax.nn.sigmoid(a)
    o_ref[...] = _dot(a.astype(BF16), w_ref[...].astype(BF16)) + b_ref[...]


def _modulation(cvec, w_mod, b_mod):
    depth, _, nm = w_mod.shape
    tn = 1024
    return pl.pallas_call(
        _mod_kernel,
        grid=(depth, nm // tn),
        in_specs=[
            pl.BlockSpec((SUB, D), lambda l, j: (0, 0)),
            pl.BlockSpec((None, D, tn), lambda l, j: (l, 0, j)),
            pl.BlockSpec((None, 1, tn), lambda l, j: (l, 0, j)),
        ],
        out_specs=pl.BlockSpec((None, SUB, tn), lambda l, j: (l, 0, j)),
        out_shape=jax.ShapeDtypeStruct((depth, SUB, nm), F32),
        compiler_params=_cp(("arbitrary", "arbitrary")),
        name="modulation",
    )(cvec, w_mod, b_mod.reshape(depth, 1, nm))


def _qkv_kernel(x_ref, g_ref, mod_ref, w_ref, hg_ref, cos_ref, sin_ref, o_ref, *, do_norm, n_rot):
    h = _rms(x_ref[...]) * g_ref[...]
    h = (h * (1.0 + mod_ref[1:2, :]) + mod_ref[0:1, :]).astype(BF16)
    cos = cos_ref[...]
    sin = sin_ref[...]
    lane = lax.broadcasted_iota(jnp.int32, cos.shape, 1)
    first = (lane % (2 * ROT_FREQS)) < ROT_FREQS
    for j in range(QKV // TN_QKV):
        cols = slice(j * TN_QKV, (j + 1) * TN_QKV)
        y = _dot(h, w_ref[:, cols])
        if j >= n_rot:
            o_ref[:, cols] = y.astype(BF16)
            continue
        outs = []
        for hh in range(TN_QKV // HD):
            yh = y[:, hh * HD:(hh + 1) * HD]
            if do_norm:
                yh = _rms(yh)
            yh = yh * hg_ref[:, j * TN_QKV + hh * HD:j * TN_QKV + (hh + 1) * HD]
            partner = jnp.where(first, pltpu.roll(yh, HD - ROT_FREQS, 1), pltpu.roll(yh, ROT_FREQS, 1))
            outs.append(yh * cos + partner * sin)
        o_ref[:, cols] = jnp.concatenate(outs, axis=1).astype(BF16)


def _qkv_proj(x_all, gain, mod, w_bf, head_gain, cos_e, sin_e, tpb, nb, do_norm):
    r = x_all.shape[0]
    n_rot = (NH + NKV) * HD // TN_QKV
    return pl.pallas_call(
        functools.partial(_qkv_kernel, do_norm=do_norm, n_rot=n_rot),
        grid=(r // TM_QKV,),
        in_specs=[
            pl.BlockSpec((TM_QKV, D), lambda i: (i, 0)),
            pl.BlockSpec((1, D), lambda i: (0, 0)),
            pl.BlockSpec((None, NMOD, D), lambda i: (jnp.minimum(i // tpb, nb), 0, 0)),
            pl.BlockSpec((D, QKV), lambda i: (0, 0)),
            pl.BlockSpec((1, QKV), lambda i: (0, 0)),
            pl.BlockSpec((TM_QKV, HD), lambda i: (i, 0)),
            pl.BlockSpec((TM_QKV, HD), lambda i: (i, 0)),
        ],
        out_specs=pl.BlockSpec((TM_QKV, QKV), lambda i: (i, 0)),
        out_shape=jax.ShapeDtypeStruct((r, QKV), BF16),
        compiler_params=_cp(("parallel",)),
        name="qkv_proj",
    )(x_all, gain, mod, w_bf, head_gain, cos_e, sin_e)


def _flash_kernel(q_ref, kc_ref, vc_ref, k_ref, v_ref, o_ref, m_scr, acc_scr):
    j = pl.program_id(3)

    def chunks(s):
        return [s[:, c * LANES:(c + 1) * LANES] for c in range(s.shape[1] // LANES)]

    def rebase_update(k, v1):
        for g in range(GQ):
            ch = chunks(_dot_nt(q_ref[:, g * HD:(g + 1) * HD], k))
            m_prev = m_scr[g]
            m_new = jnp.maximum(m_prev, jnp.max(functools.reduce(jnp.maximum, ch), axis=-1, keepdims=True))
            alpha = jnp.exp2(m_prev - m_new)
            p = jnp.concatenate([jnp.exp2(c - m_new).astype(BF16) for c in ch], axis=1)
            acc = acc_scr[g]
            acc_scr[g] = jnp.concatenate([acc[:, :HD] * alpha, acc[:, HD:] * alpha], axis=1) + _dot(p, v1)
            m_scr[g] = m_new

    @pl.when(j == 0)
    def _():
        m_scr[...] = jnp.full(m_scr.shape, NEG, F32)
        acc_scr[...] = jnp.zeros(acc_scr.shape, F32)
        vc = vc_ref[...]
        rebase_update(kc_ref[...], jnp.concatenate([vc, jnp.ones_like(vc)], axis=1))

    k = k_ref[...]
    v = v_ref[...]
    v1 = jnp.concatenate([v, jnp.ones_like(v)], axis=1)
    ps = []
    excess = None
    for g in range(GQ):
        ch = chunks(_dot_nt(q_ref[:, g * HD:(g + 1) * HD], k))
        m = m_scr[g]
        over = functools.reduce(jnp.maximum, ch) - m
        excess = over if excess is None else jnp.maximum(excess, over)
        ps.append(jnp.concatenate([jnp.exp2(c - m).astype(BF16) for c in ch], axis=1))
    worst = jnp.max(jnp.max(excess, axis=0, keepdims=True), axis=1, keepdims=True)[0, 0]

    @pl.when(worst <= FLASH_MARGIN)
    def _():
        for g in range(GQ):
            acc_scr[g] += _dot(ps[g], v1)

    @pl.when(worst > FLASH_MARGIN)
    def _():
        rebase_update(k, v1)

    @pl.when(j == pl.num_programs(3) - 1)
    def _():
        for g in range(GQ):
            acc = acc_scr[g]
            o_ref[:, g * HD:(g + 1) * HD] = (acc[:, :HD] / acc[:, HD:]).astype(BF16)


def _global_attention(qkv, nb, s, c):
    nq = s // TQ_FLASH
    nk = s // TK_FLASH
    kcol = NH
    vcol = NH + NKV
    cblk = nb * s // c
    return pl.pallas_call(
        _flash_kernel,
        grid=(nb, NKV, nq, nk),
        in_specs=[
            pl.BlockSpec((TQ_FLASH, GQ * HD), lambda b, h, i, j: (b * nq + i, h)),
            pl.BlockSpec((c, HD), lambda b, h, i, j: (cblk + b, kcol + h)),
            pl.BlockSpec((c, HD), lambda b, h, i, j: (cblk + b, vcol + h)),
            pl.BlockSpec((TK_FLASH, HD), lambda b, h, i, j: (b * nk + j, kcol + h)),
            pl.BlockSpec((TK_FLASH, HD), lambda b, h, i, j: (b * nk + j, vcol + h)),
        ],
        out_specs=pl.BlockSpec((TQ_FLASH, GQ * HD), lambda b, h, i, j: (b * nq + i, h)),
        out_shape=jax.ShapeDtypeStruct((nb * s, NH * HD), BF16),
        scratch_shapes=[
            pltpu.VMEM((GQ, TQ_FLASH, LANES), F32),
            pltpu.VMEM((GQ, TQ_FLASH, 2 * HD), F32),
        ],
        compiler_params=_cp(("parallel", "parallel", "parallel", "arbitrary")),
        name="global_attention",
    )(qkv, qkv, qkv, qkv, qkv)


def _window_kernel(sink_ref, q_ref, kc_ref, vc_ref, kp_ref, vp_ref, k0_ref, v0_ref, kn_ref, vn_ref, bias_ref, o_ref):
    h = pl.program_id(1)
    k = jnp.concatenate([kc_ref[...], kp_ref[...], k0_ref[...], kn_ref[...]], axis=0)
    v = jnp.concatenate([vc_ref[...], vp_ref[...], v0_ref[...], vn_ref[...]], axis=0)
    v1 = jnp.concatenate([v, jnp.ones_like(v)], axis=1)
    bias = bias_ref[...]
    for g in range(GQ):
        sink = sink_ref[h * GQ + g] * LOG2E
        s = _dot_nt(q_ref[:, g * HD:(g + 1) * HD], k) + bias
        m = jnp.maximum(jnp.max(s, axis=-1, keepdims=True), sink)
        acc = _dot(jnp.exp2(s - m).astype(BF16), v1)
        o_ref[:, g * HD:(g + 1) * HD] = (acc[:, :HD] / (acc[:, HD:] + jnp.exp2(sink - m))).astype(BF16)


def _window_bias(c):
    q = jnp.arange(TQ_WIN, dtype=jnp.int32)[:, None]
    kk = jnp.arange(WINDOW, dtype=jnp.int32)[None, :]
    before = kk >= q
    after = q >= TQ_WIN - WINDOW + kk
    inside = jnp.abs(q - jnp.arange(TQ_WIN, dtype=jnp.int32)[None, :]) <= WINDOW
    ctx = jnp.ones((TQ_WIN, c), bool)
    variants = []
    for var in range(4):
        ok = jnp.concatenate([ctx, before & ((var & 1) == 0), inside, after & ((var & 2) == 0)], axis=1)
        variants.append(jnp.where(ok, 0.0, NEG).astype(F32))
    return jnp.stack(variants)


def _window_attention(qkv, sink, nb, s, c):
    tq = TQ_WIN
    nq = s // tq
    per = tq // WINDOW
    nw = s // WINDOW
    kcol = NH
    vcol = NH + NKV
    cblk = nb * s // c

    def edge(col, off):
        return pl.BlockSpec((WINDOW, HD), lambda b, h, i: (b * nw + jnp.clip(i * per + off, 0, nw - 1), col + h))

    def tile(col):
        return pl.BlockSpec((tq, HD), lambda b, h, i: (b * nq + i, col + h))

    return pl.pallas_call(
        _window_kernel,
        grid=(nb, NKV, nq),
        in_specs=[
            pl.BlockSpec(memory_space=pltpu.SMEM),
            pl.BlockSpec((tq, GQ * HD), lambda b, h, i: (b * nq + i, h)),
            pl.BlockSpec((c, HD), lambda b, h, i: (cblk + b, kcol + h)),
            pl.BlockSpec((c, HD), lambda b, h, i: (cblk + b, vcol + h)),
            edge(kcol, -1), edge(vcol, -1), tile(kcol), tile(vcol), edge(kcol, per), edge(vcol, per),
            pl.BlockSpec((None, tq, c + 2 * WINDOW + tq),
                         lambda b, h, i: ((i == 0).astype(jnp.int32) + 2 * (i == nq - 1).astype(jnp.int32), 0, 0)),
        ],
        out_specs=pl.BlockSpec((tq, GQ * HD), lambda b, h, i: (b * nq + i, h)),
        out_shape=jax.ShapeDtypeStruct((nb * s, NH * HD), BF16),
        compiler_params=_cp(("parallel", "parallel", "parallel")),
        name="window_attention",
    )(sink, qkv, qkv, qkv, qkv, qkv, qkv, qkv, qkv, qkv, _window_bias(c))


def _ctx_attn_kernel(sink_ref, q_ref, k_ref, v_ref, o_ref, *, use_sink):
    h = pl.program_id(1)
    k = k_ref[...]
    v = v_ref[...]
    for g in range(GQ):
        s = _dot_nt(q_ref[:, g * HD:(g + 1) * HD], k)
        m = jnp.max(s, axis=-1, keepdims=True)
        if use_sink:
            sink = sink_ref[h * GQ + g] * LOG2E
            m = jnp.maximum(m, sink)
        p = jnp.exp2(s - m)
        l = jnp.sum(p, axis=-1, keepdims=True)
        if use_sink:
            l = l + jnp.exp2(sink - m)
        o_ref[:, g * HD:(g + 1) * HD] = (_dot(p.astype(BF16), v) / l).astype(BF16)


def _ctx_attention(qkv, sink, nb, s, c, use_sink):
    cblk = nb * s // c
    return pl.pallas_call(
        functools.partial(_ctx_attn_kernel, use_sink=use_sink),
        grid=(nb, NKV),
        in_specs=[
            pl.BlockSpec(memory_space=pltpu.SMEM),
            pl.BlockSpec((c, GQ * HD), lambda b, h: (cblk + b, h)),
            pl.BlockSpec((c, HD), lambda b, h: (cblk + b, NH + h)),
            pl.BlockSpec((c, HD), lambda b, h: (cblk + b, NH + NKV + h)),
        ],
        out_specs=pl.BlockSpec((c, GQ * HD), lambda b, h: (b, h)),
        out_shape=jax.ShapeDtypeStruct((nb * c, NH * HD), BF16),
        compiler_params=_cp(("arbitrary", "arbitrary")),
        name="ctx_attention",
    )(sink, qkv, qkv, qkv)


def _oproj_kernel(a_ref, ac_ref, w_ref, x_ref, g_ref, mod_ref, o_ref, *, nlat):
    def project(src_ref):
        y = _dot(src_ref[...].astype(BF16), w_ref[...])
        o_ref[...] = x_ref[...] + mod_ref[2:3, :] * (_rms(y) * g_ref[...])

    i = pl.program_id(0)

    @pl.when(i < nlat)
    def _():
        project(a_ref)

    @pl.when(i >= nlat)
    def _():
        project(ac_ref)


def _oproj(a_lat, a_ctx, w_bf, x_all, gain, mod, tpb, nb, with_ctx):
    tm = TM_OPROJ
    nlat = a_lat.shape[0] // tm
    ntiles = nlat + (a_ctx.shape[0] // tm if with_ctx else 0)
    return pl.pallas_call(
        functools.partial(_oproj_kernel, nlat=nlat),
        grid=(ntiles,),
        in_specs=[
            pl.BlockSpec((tm, D), lambda i: (jnp.minimum(i, nlat - 1), 0)),
            pl.BlockSpec((tm, D), lambda i: (jnp.maximum(i - nlat, 0), 0)),
            pl.BlockSpec((D, D), lambda i: (0, 0)),
            pl.BlockSpec((tm, D), lambda i: (i, 0)),
            pl.BlockSpec((1, D), lambda i: (0, 0)),
            pl.BlockSpec((None, NMOD, D), lambda i: (jnp.minimum(i // tpb, nb), 0, 0)),
        ],
        out_specs=pl.BlockSpec((tm, D), lambda i: (i, 0)),
        out_shape=jax.ShapeDtypeStruct(x_all.shape, F32),
        input_output_aliases={3: 0},
        compiler_params=_cp(("parallel",)),
        name="oproj",
    )(a_lat, a_ctx, w_bf, x_all, gain, mod)


def _dft_mats(n):
    k = np.arange(n)
    ang = 2.0 * np.pi * ((k[:, None] * k[None, :]) % n) / n
    return np.cos(ang) / math.sqrt(n), np.sin(ang) / math.sqrt(n)


def _stage1_matrix(n1):
    c, s = _dft_mats(n1)
    eye = np.eye(SUB)
    return np.concatenate([np.kron(c, eye), -np.kron(s, eye)], axis=0)


def _stage2_matrix():
    c, s = _dft_mats(N2)
    return np.block([[c, s], [-s, c]])


def _channel_matrix():
    c, s = _dft_mats(FC)
    return np.concatenate([c, s], axis=0)


def _fft1_kernel(x_ref, g_ref, mod_ref, kr_ref, tc_ref, ts_ref, ar_ref, ai_ref, rstd_scr):
    dt = pl.program_id(2)
    n1 = x_ref.shape[0]
    rows = n1 * SUB

    @pl.when(dt == 0)
    def _():
        x = x_ref[...].reshape(rows, D)
        rstd_scr[...] = lax.rsqrt(jnp.mean(x * x, axis=-1, keepdims=True) + EPS)

    col = pl.multiple_of(dt * TD_F, TD_F)
    x = x_ref[:, :, pl.ds(col, TD_F)].reshape(rows, TD_F)
    h = x * rstd_scr[...] * g_ref[...]
    h = h * (1.0 + mod_ref[1:2, :]) + mod_ref[0:1, :]
    y = _dot(kr_ref[...], h.astype(BF16))
    yr = y[:rows]
    yi = y[rows:]
    tc = tc_ref[...]
    ts = ts_ref[...]
    for cc in range(TD_F // LANES):
        sl = slice(cc * LANES, (cc + 1) * LANES)
        a = yr[:, sl]
        b = yi[:, sl]
        ar_ref[:, :, sl] = (a * tc + b * ts).reshape(n1, SUB, LANES)
        ai_ref[:, :, sl] = (b * tc - a * ts).reshape(n1, SUB, LANES)


def _fft2_kernel(ar_ref, ai_ref, w2_ref, cs_ref, o_ref):
    yr, yi = [], []
    for jj in range(SUB):
        rows = slice(jj * N2, (jj + 1) * N2)
        a = jnp.concatenate([ar_ref[rows, :], ai_ref[rows, :]], axis=0).astype(BF16)
        y = _dot(w2_ref[...], a)
        yr.append(y[:N2])
        yi.append(y[N2:])
    yr = jnp.concatenate(yr, axis=0).astype(BF16)
    yi = jnp.concatenate(yi, axis=0).astype(BF16)
    for gi in range(TD_F // FC):
        sl = slice(gi * FC, (gi + 1) * FC)
        res = _dot(jnp.concatenate([yr[:, sl], yi[:, sl]], axis=1), cs_ref[...])
        for jj in range(SUB):
            o_ref[:, jj, sl] = res[jj * N2:(jj + 1) * N2]


def _fnet_ctx_kernel(x_ref, g_ref, mod_ref, f_ref, cs_ref, o_ref):
    c = x_ref.shape[0]
    h = _rms(x_ref[...]) * g_ref[...]
    h = h * (1.0 + mod_ref[1:2, :]) + mod_ref[0:1, :]
    y = _dot(f_ref[...], h.astype(BF16))
    yr = y[:c].astype(BF16)
    yi = y[c:].astype(BF16)
    for gi in range(FG):
        sl = slice(gi * FC, (gi + 1) * FC)
        z = jnp.concatenate([yr[:, sl], yi[:, sl]], axis=1)
        o_ref[:, sl] = _dot(z, cs_ref[...])


def _fourier_mix(x_all, gain, mod, nb, s, c, with_ctx):
    r = x_all.shape[0]
    n1 = s // N2
    nch = N2 // SUB
    ndt = D // TD_F
    kr = jnp.asarray(_stage1_matrix(n1), BF16)
    k2 = jnp.asarray(_stage2_matrix(), BF16)
    cs = jnp.asarray(_channel_matrix(), BF16)
    rows = n1 * SUB
    k1 = jnp.arange(n1, dtype=jnp.int32)[None, :, None]
    n2 = (jnp.arange(nch, dtype=jnp.int32)[:, None, None] * SUB + jnp.arange(SUB, dtype=jnp.int32)[None, None, :])
    ang = (2.0 * np.pi / s) * ((k1 * n2) % s).astype(F32).reshape(nch, rows, 1)
    tc = jnp.broadcast_to(jnp.cos(ang), (nch, rows, LANES))
    ts = jnp.broadcast_to(jnp.sin(ang), (nch, rows, LANES))

    x4 = x_all.reshape(r // N2, N2, D)
    ar, ai = pl.pallas_call(
        _fft1_kernel,
        grid=(nb, nch, ndt),
        in_specs=[
            pl.BlockSpec((n1, SUB, D), lambda b, ch, dt: (b, ch, 0)),
            pl.BlockSpec((1, TD_F), lambda b, ch, dt: (0, dt)),
            pl.BlockSpec((None, NMOD, TD_F), lambda b, ch, dt: (b, 0, dt)),
            pl.BlockSpec((2 * rows, rows), lambda b, ch, dt: (0, 0)),
            pl.BlockSpec((None, rows, LANES), lambda b, ch, dt: (ch, 0, 0)),
            pl.BlockSpec((None, rows, LANES), lambda b, ch, dt: (ch, 0, 0)),
        ],
        out_specs=[
            pl.BlockSpec((n1, SUB, TD_F), lambda b, ch, dt: (b, ch, dt)),
            pl.BlockSpec((n1, SUB, TD_F), lambda b, ch, dt: (b, ch, dt)),
        ],
        out_shape=[jax.ShapeDtypeStruct((nb * n1, N2, D), F32)] * 2,
        scratch_shapes=[pltpu.VMEM((rows, 1), F32)],
        compiler_params=_cp(("parallel", "parallel", "arbitrary")),
        name="fnet_stage1",
    )(x4, gain, mod, kr, tc, ts)

    ar2 = ar.reshape(nb * s, D)
    ai2 = ai.reshape(nb * s, D)
    nkc = n1 // SUB
    mixed = pl.pallas_call(
        _fft2_kernel,
        grid=(nb, nkc, ndt),
        in_specs=[
            pl.BlockSpec((N2 * SUB, TD_F), lambda b, kc, dt: (b * nkc + kc, dt)),
            pl.BlockSpec((N2 * SUB, TD_F), lambda b, kc, dt: (b * nkc + kc, dt)),
            pl.BlockSpec((2 * N2, 2 * N2), lambda b, kc, dt: (0, 0)),
            pl.BlockSpec((2 * FC, FC), lambda b, kc, dt: (0, 0)),
        ],
        out_specs=pl.BlockSpec((N2, SUB, TD_F), lambda b, kc, dt: (b, kc, dt)),
        out_shape=jax.ShapeDtypeStruct((nb * s // n1, n1, D), F32),
        compiler_params=_cp(("parallel", "parallel", "parallel")),
        name="fnet_stage2",
    )(ar2, ai2, k2, cs)
    mixed = mixed.reshape(nb * s, D)

    mixed_ctx = None
    if with_ctx:
        fc, fs = _dft_mats(c)
        fmat = jnp.asarray(np.concatenate([fc, -fs], axis=0), BF16)
        cblk = nb * s // c
        mixed_ctx = pl.pallas_call(
            _fnet_ctx_kernel,
            grid=(nb,),
            in_specs=[
                pl.BlockSpec((c, D), lambda b: (cblk + b, 0)),
                pl.BlockSpec((1, D), lambda b: (0, 0)),
                pl.BlockSpec((None, NMOD, D), lambda b: (nb, 0, 0)),
                pl.BlockSpec((2 * c, c), lambda b: (0, 0)),
                pl.BlockSpec((2 * FC, FC), lambda b: (0, 0)),
            ],
            out_specs=pl.BlockSpec((c, D), lambda b: (b, 0)),
            out_shape=jax.ShapeDtypeStruct((nb * c, D), F32),
            compiler_params=_cp(("arbitrary",)),
            name="fnet_ctx",
        )(x_all, gain, mod, fmat, cs)
    return mixed, mixed_ctx


M_E1, M_E2, M_R1, M_R2, M_W1, M_W2, M_PG = range(7)
LANE_E0 = NG


def _ffn_tokens(x_ref, g_ref, mod_ref):
    tok = _rms(x_ref[...]) * g_ref[...]
    return tok * (1.0 + mod_ref[4:5, :]) + mod_ref[3:4, :]


def _route_kernel(x_ref, g_ref, mod_ref, wr_ref, br_ref, meta_ref, cnt_ref, base_scr):
    i = pl.program_id(0)
    tm = x_ref.shape[0]

    @pl.when(i == 0)
    def _():
        base_scr[...] = jnp.zeros(base_scr.shape, F32)

    tok = _ffn_tokens(x_ref, g_ref, mod_ref)
    wr = wr_ref[...]
    w_hi = wr.astype(BF16)
    w_lo = (wr - w_hi.astype(F32)).astype(BF16)
    t_hi = tok.astype(BF16)
    t_lo = (tok - t_hi.astype(F32)).astype(BF16)
    logits = _dot(t_hi, w_hi) + (_dot(t_hi, w_lo) + _dot(t_lo, w_hi)) + br_ref[...]

    lane = lax.broadcasted_iota(jnp.int32, logits.shape, 1)
    big = jnp.int32(LANES)
    gmask = lane < NG
    gl = jnp.where(gmask, logits, NEG)
    gmax = jnp.max(gl, axis=-1, keepdims=True)
    gsum = jnp.sum(jnp.where(gmask, jnp.exp(gl - gmax), 0.0), axis=-1, keepdims=True)
    gidx = jnp.min(jnp.where(gl == gmax, lane, big), axis=-1, keepdims=True)
    lo = LANE_E0 + gidx * EPG
    emask = (lane >= lo) & (lane < lo + EPG)
    el = jnp.where(emask, logits, NEG)
    emax = jnp.max(el, axis=-1, keepdims=True)
    idx1 = jnp.min(jnp.where(el == emax, lane, big), axis=-1, keepdims=True)
    el2 = jnp.where(lane == idx1, NEG, el)
    e2max = jnp.max(el2, axis=-1, keepdims=True)
    idx2 = jnp.min(jnp.where(el2 == e2max, lane, big), axis=-1, keepdims=True)
    t = jnp.exp(e2max - emax)
    w1 = 1.0 / (1.0 + t)
    w2 = t / (1.0 + t)

    oh1 = (lane == idx1).astype(F32)
    oh2 = (lane == idx2).astype(F32)
    rr = lax.broadcasted_iota(jnp.int32, (tm, tm), 0)
    cc = lax.broadcasted_iota(jnp.int32, (tm, tm), 1)
    tri = (cc < rr).astype(BF16)
    base = base_scr[0:1, :]
    cnt1 = jnp.sum(oh1, axis=0, keepdims=True)
    cnt2 = jnp.sum(oh2, axis=0, keepdims=True)
    pre1 = _dot(tri, oh1.astype(BF16)) + base
    pre2 = _dot(tri, oh2.astype(BF16)) + base + cnt1
    r1 = jnp.sum(oh1 * pre1, axis=-1, keepdims=True)
    r2 = jnp.sum(oh2 * pre2, axis=-1, keepdims=True)
    new_base = base + cnt1 + cnt2
    base_scr[0:1, :] = new_base
    cnt_ref[...] = jnp.broadcast_to(new_base, cnt_ref.shape)

    rec = jnp.zeros(logits.shape, F32)
    for k, val in ((M_E1, (idx1 - LANE_E0).astype(F32)), (M_E2, (idx2 - LANE_E0).astype(F32)),
                   (M_R1, r1), (M_R2, r2), (M_W1, w1), (M_W2, w2), (M_PG, 1.0 / gsum)):
        rec = jnp.where(lane == k, val, rec)
    meta_ref[...] = rec


def _pack_pairs(t):
    half = t.shape[1] // 2
    hi = lax.bitcast_convert_type(t[:, :half].astype(BF16).astype(F32), U32)
    lo = lax.bitcast_convert_type(t[:, half:].astype(BF16).astype(F32), U32)
    return hi | (lo >> 16)


def _unpack_pairs(w):
    hi = lax.bitcast_convert_type(w & jnp.uint32(0xFFFF0000), F32)
    lo = lax.bitcast_convert_type(w << 16, F32)
    return jnp.concatenate([hi, lo], axis=1)


def _row_copy_wait(hbm_ref, vmem_ref, sem):
    pltpu.make_async_copy(hbm_ref.at[pl.ds(0, vmem_ref.shape[0])], vmem_ref, sem).wait()


def _dispatch_kernel(pt_ref, d_ref, x_ref, g_ref, mod_ref, xs_hbm, buf, zbuf, sem, zsem):
    i = pl.program_id(0)
    nt = pl.num_programs(0)
    tm = x_ref.shape[0]
    slot = i % 2

    @pl.when(i == 0)
    def _():
        zbuf[...] = jnp.zeros(zbuf.shape, U32)

        def zero_tile(t, carry):
            @pl.when(pt_ref[t] != 0)
            def _():
                pltpu.make_async_copy(zbuf, xs_hbm.at[pl.ds(pl.multiple_of(t * TMX, TMX), TMX)], zsem).start()
            return carry

        def zero_wait(t, carry):
            @pl.when(pt_ref[t] != 0)
            def _():
                _row_copy_wait(xs_hbm, zbuf, zsem)
            return carry

        lax.fori_loop(0, pt_ref.shape[0], zero_tile, 0)
        lax.fori_loop(0, pt_ref.shape[0], zero_wait, 0)

    @pl.when(i >= 2)
    def _():
        _row_copy_wait(xs_hbm, buf.at[slot], sem.at[slot])
        _row_copy_wait(xs_hbm, buf.at[slot], sem.at[slot])

    buf[slot] = _pack_pairs(_ffn_tokens(x_ref, g_ref, mod_ref))

    def issue(rw, carry):
        src = buf.at[slot, pl.ds(rw, 1)]
        pltpu.make_async_copy(src, xs_hbm.at[pl.ds(d_ref[0, rw], 1)], sem.at[slot]).start()
        pltpu.make_async_copy(src, xs_hbm.at[pl.ds(d_ref[0, tm + rw], 1)], sem.at[slot]).start()
        return carry

    for rw in range(tm):
        issue(rw, 0)

    @pl.when(i == nt - 1)
    def _():
        _row_copy_wait(xs_hbm, buf.at[slot], sem.at[slot])
        _row_copy_wait(xs_hbm, buf.at[slot], sem.at[slot])

        @pl.when(nt >= 2)
        def _():
            _row_copy_wait(xs_hbm, buf.at[1 - slot], sem.at[1 - slot])
            _row_copy_wait(xs_hbm, buf.at[1 - slot], sem.at[1 - slot])


def _expert_kernel(te_ref, nv_ref, xs_ref, wg_ref, wu_ref, wd_ref, o_ref, wg_s, wu_s, wd_s):
    t = pl.program_id(0)

    @pl.when(t < nv_ref[0])
    def _():
        prev = te_ref[jnp.maximum(t - 1, 0)]

        @pl.when((t == 0) | (te_ref[t] != prev))
        def _():
            wg_s[...] = wg_ref[...].astype(BF16)
            wu_s[...] = wu_ref[...].astype(BF16)
            wd_s[...] = wd_ref[...].astype(BF16)

        x = _unpack_pairs(xs_ref[...]).astype(BF16)
        a = _dot(x, wg_s[...])
        u = _dot(x, wu_s[...])
        hmid = (a * jax.nn.sigmoid(a) * u).astype(BF16)
        o_ref[...] = _pack_pairs(_dot(hmid, wd_s[...]))

    @pl.when(t >= nv_ref[0])
    def _():
        o_ref[...] = jnp.zeros(o_ref.shape, U32)


def _combine_kernel(d_ref, dn_ref, meta_ref, ys_hbm, x_ref, g_ref, mod_ref, o_ref, buf, sem):
    i = pl.program_id(0)
    nt = pl.num_programs(0)
    tm = x_ref.shape[0]
    slot = i % 2

    def gather(idx_ref, sl):
        def issue(rw, carry):
            pltpu.make_async_copy(ys_hbm.at[pl.ds(idx_ref[0, rw], 1)], buf.at[sl, 0, pl.ds(rw, 1)], sem.at[sl]).start()
            pltpu.make_async_copy(ys_hbm.at[pl.ds(idx_ref[0, tm + rw], 1)], buf.at[sl, 1, pl.ds(rw, 1)], sem.at[sl]).start()
            return carry

        for rw in range(tm):
            issue(rw, 0)

    @pl.when(i == 0)
    def _():
        gather(d_ref, 0)

    @pl.when(i + 1 < nt)
    def _():
        gather(dn_ref, 1 - slot)

    _row_copy_wait(ys_hbm, buf.at[slot, 0], sem.at[slot])
    _row_copy_wait(ys_hbm, buf.at[slot, 1], sem.at[slot])
    meta = meta_ref[...]
    w1 = meta[:, M_W1:M_W1 + 1]
    w2 = meta[:, M_W2:M_W2 + 1]
    pg = meta[:, M_PG:M_PG + 1]
    y = w1 * _unpack_pairs(buf[slot, 0]) + w2 * _unpack_pairs(buf[slot, 1])
    o_ref[...] = x_ref[...] + mod_ref[5:6, :] * (pg * (_rms(y) * g_ref[...]))


def _moe(x_all, layer, pre_gain, post_gain, mod, wr, br, w_gate, w_up, w_down, tpb, nb, nrows, out_rows):
    tm = TM_MOE
    nt = nrows // tm
    mod_map = lambda i: (jnp.minimum(i // tpb, nb), 0, 0)
    meta, cnt = pl.pallas_call(
        _route_kernel,
        grid=(nt,),
        in_specs=[
            pl.BlockSpec((tm, D), lambda i: (i, 0)),
            pl.BlockSpec((1, D), lambda i: (0, 0)),
            pl.BlockSpec((None, NMOD, D), mod_map),
            pl.BlockSpec((D, LANES), lambda i: (0, 0)),
            pl.BlockSpec((1, LANES), lambda i: (0, 0)),
        ],
        out_specs=[
            pl.BlockSpec((tm, LANES), lambda i: (i, 0)),
            pl.BlockSpec((SUB, LANES), lambda i: (0, 0)),
        ],
        out_shape=[
            jax.ShapeDtypeStruct((nrows, LANES), F32),
            jax.ShapeDtypeStruct((SUB, LANES), F32),
        ],
        scratch_shapes=[pltpu.VMEM((SUB, LANES), F32)],
        compiler_params=_cp(("arbitrary",)),
        name="moe_route",
    )(x_all, pre_gain, mod, wr, br)

    ntile = nrows * 2 // TMX + NE
    nslot = ntile * TMX
    e = meta[:, M_E1:M_E2 + 1].astype(jnp.int32)
    rk = meta[:, M_R1:M_R2 + 1].astype(jnp.int32)
    counts = cnt[0, LANE_E0:LANE_E0 + NE].astype(jnp.int32)
    padded = (counts + TMX - 1) // TMX * TMX
    seg_end = jnp.cumsum(padded)
    seg_start = seg_end - padded
    onehot = e[:, :, None] == jnp.arange(NE, dtype=jnp.int32)[None, None, :]
    dest = jnp.sum(jnp.where(onehot, seg_start[None, None, :], 0), axis=-1) + rk
    dtile = dest.reshape(nt, tm, 2).transpose(0, 2, 1).reshape(nt, 1, 2 * tm)
    nvalid = (seg_end[-1] // TMX).astype(jnp.int32).reshape(1)
    tile_id = jnp.arange(ntile, dtype=jnp.int32)
    te_raw = jnp.sum((tile_id[:, None] * TMX >= seg_end[None, :]).astype(jnp.int32), axis=1)
    te = jnp.minimum(te_raw, te_raw[jnp.maximum(nvalid[0] - 1, 0)])
    last_of_seg = jnp.concatenate([te_raw[1:] != te_raw[:-1], jnp.ones((1,), bool)])
    partial = (last_of_seg | (tile_id >= nvalid[0])).astype(jnp.int32)

    xs = pl.pallas_call(
        _dispatch_kernel,
        grid_spec=pltpu.PrefetchScalarGridSpec(
            num_scalar_prefetch=1,
            grid=(nt,),
            in_specs=[
                pl.BlockSpec((None, 1, 2 * tm), lambda i, pt: (i, 0, 0), memory_space=pltpu.SMEM),
                pl.BlockSpec((tm, D), lambda i, pt: (i, 0)),
                pl.BlockSpec((1, D), lambda i, pt: (0, 0)),
                pl.BlockSpec((None, NMOD, D), lambda i, pt: mod_map(i)),
            ],
            out_specs=pl.BlockSpec(memory_space=pl.ANY),
            scratch_shapes=[
                pltpu.VMEM((2, tm, DP), U32),
                pltpu.VMEM((TMX, DP), U32),
                pltpu.SemaphoreType.DMA((2,)),
                pltpu.SemaphoreType.DMA(()),
            ],
        ),
        out_shape=jax.ShapeDtypeStruct((nslot, DP), U32),
        compiler_params=_cp(("arbitrary",)),
        name="moe_dispatch",
    )(partial, dtile, x_all, pre_gain, mod)

    ys = pl.pallas_call(
        _expert_kernel,
        grid_spec=pltpu.PrefetchScalarGridSpec(
            num_scalar_prefetch=2,
            grid=(ntile,),
            in_specs=[
                pl.BlockSpec((TMX, DP), lambda t, te_r, nv_r: (jnp.minimum(t, nv_r[0] - 1), 0)),
                pl.BlockSpec((None, None, D, FF), lambda t, te_r, nv_r: (layer, te_r[t], 0, 0)),
                pl.BlockSpec((None, None, D, FF), lambda t, te_r, nv_r: (layer, te_r[t], 0, 0)),
                pl.BlockSpec((None, None, FF, D), lambda t, te_r, nv_r: (layer, te_r[t], 0, 0)),
            ],
            out_specs=pl.BlockSpec((TMX, DP), lambda t, te_r, nv_r: (t, 0)),
            scratch_shapes=[
                pltpu.VMEM((D, FF), BF16),
                pltpu.VMEM((D, FF), BF16),
                pltpu.VMEM((FF, D), BF16),
            ],
        ),
        out_shape=jax.ShapeDtypeStruct((nslot, DP), U32),
        compiler_params=_cp(("arbitrary",)),
        name="moe_experts",
    )(te, nvalid, xs, w_gate, w_up, w_down)

    alias = {} if out_rows != x_all.shape[0] else {4: 0}
    return pl.pallas_call(
        _combine_kernel,
        grid=(nt,),
        in_specs=[
            pl.BlockSpec((None, 1, 2 * tm), lambda i: (i, 0, 0), memory_space=pltpu.SMEM),
            pl.BlockSpec((None, 1, 2 * tm), lambda i: (jnp.minimum(i + 1, nt - 1), 0, 0), memory_space=pltpu.SMEM),
            pl.BlockSpec((tm, LANES), lambda i: (i, 0)),
            pl.BlockSpec(memory_space=pl.ANY),
            pl.BlockSpec((tm, D), lambda i: (i, 0)),
            pl.BlockSpec((1, D), lambda i: (0, 0)),
            pl.BlockSpec((None, NMOD, D), mod_map),
        ],
        out_specs=pl.BlockSpec((tm, D), lambda i: (i, 0)),
        out_shape=jax.ShapeDtypeStruct((out_rows, D), F32),
        scratch_shapes=[pltpu.VMEM((2, 2, tm, DP), U32), pltpu.SemaphoreType.DMA((2,))],
        input_output_aliases=alias,
        compiler_params=_cp(("arbitrary",)),
        name="moe_combine",
    )(dtile, dtile, meta, ys, x_all, post_gain, mod)


def _rope_tables(nb, s, c):
    pos = jnp.arange(s, dtype=jnp.int32)
    row = (pos // GRID_W).astype(F32)
    colp = (pos % GRID_W).astype(F32)
    inv_freq = ROPE_THETA ** (-jnp.arange(ROT_FREQS, dtype=F32) / ROT_FREQS)
    ar = row[:, None] * inv_freq
    ac = colp[:, None] * inv_freq
    cos = jnp.concatenate([jnp.cos(ar), jnp.cos(ar), jnp.cos(ac), jnp.cos(ac)], axis=1)
    sin = jnp.concatenate([-jnp.sin(ar), jnp.sin(ar), -jnp.sin(ac), jnp.sin(ac)], axis=1)
    cos = jnp.concatenate([jnp.tile(cos, (nb, 1)), jnp.ones((nb * c, HD), F32)], axis=0)
    sin = jnp.concatenate([jnp.tile(sin, (nb, 1)), jnp.zeros((nb * c, HD), F32)], axis=0)
    return cos, sin


def kernel(x, c, ctx, c_ctx, pre_norm_mix, post_norm_mix, pre_norm_ffn, post_norm_ffn, w_mod, b_mod, fnet_w_out, gqa_w_qkv, gqa_q_norm, gqa_k_norm, gqa_w_o, swa_w_qkv, swa_sink, swa_w_o, router_group_w, router_group_b, router_expert_w, router_expert_b, expert_w_gate, expert_w_up, expert_w_down):
    nb, s, _ = x.shape
    cl = ctx.shape[1]
    depth = w_mod.shape[0]
    r_lat = nb * s
    r_all = r_lat + nb * cl
    assert s % TQ_FLASH == 0 and s % (N2 * SUB) == 0 and (nb * cl) % TM_QKV == 0 and r_all % (s // N2) == 0

    x_all = jnp.concatenate([x.reshape(r_lat, D), ctx.reshape(nb * cl, D)], axis=0)
    cvec = jnp.zeros((SUB, D), F32).at[:nb].set(c).at[nb].set(c_ctx)
    mods = _modulation(cvec, w_mod, b_mod)[:, :nb + 1].reshape(depth, nb + 1, NMOD, D)
    cos_e, sin_e = _rope_tables(nb, s, cl)

    for i in range(depth):
        kind = i % 3
        j = i // 3
        ctx_out = i < depth - 1
        mod = mods[i]
        g_pre = pre_norm_mix[i].reshape(1, D)
        g_post = post_norm_mix[i].reshape(1, D)

        if kind == 0:
            mixed, mixed_ctx = _fourier_mix(x_all, g_pre, mod, nb, s, cl, ctx_out)
            w_out = fnet_w_out[j].astype(BF16)
        else:
            if kind == 1:
                w_qkv, w_out = gqa_w_qkv[j], gqa_w_o[j].astype(BF16)
                qg, kg = gqa_q_norm[j], gqa_k_norm[j]
            else:
                w_qkv, w_out = swa_w_qkv[j], swa_w_o[j].astype(BF16)
                qg = kg = jnp.ones((HD,), F32)
            head_gain = jnp.concatenate([jnp.tile(qg * (ATTN_SCALE * LOG2E), NH), jnp.tile(kg, NKV), jnp.ones((NKV * HD,), F32)])
            head_gain = head_gain.reshape(1, QKV)
            qkv = _qkv_proj(x_all, g_pre, mod, w_qkv.astype(BF16), head_gain, cos_e, sin_e,
                            s // TM_QKV, nb, do_norm=(kind == 1))
            if kind == 1:
                mixed = _global_attention(qkv, nb, s, cl)
                sink = jnp.zeros((NH,), F32)
            else:
                sink = swa_sink[j]
                mixed = _window_attention(qkv, sink, nb, s, cl)
            mixed_ctx = _ctx_attention(qkv, sink, nb, s, cl, use_sink=(kind == 2)) if ctx_out else None

        rows = r_all if ctx_out else r_lat
        x_all = _oproj(mixed, mixed_ctx if ctx_out else mixed, w_out, x_all, g_post, mod,
                       s // TM_OPROJ, nb, ctx_out)

        wr = jnp.zeros((D, LANES), F32).at[:, :NG].set(router_group_w[i]).at[:, LANE_E0:LANE_E0 + NE].set(router_expert_w[i])
        br = jnp.zeros((1, LANES), F32).at[0, :NG].set(router_group_b[i]).at[0, LANE_E0:LANE_E0 + NE].set(router_expert_b[i])
        x_all = _moe(x_all, i, pre_norm_ffn[i].reshape(1, D), post_norm_ffn[i].reshape(1, D), mod, wr, br,
                     expert_w_gate, expert_w_up, expert_w_down, s // TM_MOE, nb, rows,
                     r_all if ctx_out else r_lat)
    return x_all.reshape(nb, s, D)
```

```python
import functools
import math

import numpy as np
import jax
import jax.numpy as jnp
from jax import lax
from jax.experimental import pallas as pl
from jax.experimental.pallas import tpu as pltpu

F32 = jnp.float32
BF16 = jnp.bfloat16
U32 = jnp.uint32

D = 2048
DP = D // 2
HD = 128
NH = 16
NKV = 4
GQ = NH // NKV
QKV = (NH + 2 * NKV) * HD
GRID_W = 64
ROPE_THETA = 10000.0
ROT_FREQS = HD // 4
WINDOW = 128
FG = 8
FC = D // FG
NG = 4
EPG = 8
NE = NG * EPG
FF = 512
NMOD = 6
EPS = 1e-6
ATTN_SCALE = HD ** -0.5
DEPTH = 4

LANES = 128
SUB = 8
VMEM_LIMIT = 56 * 1024 * 1024
VMEM_LIMIT_FLASH = 62 * 1024 * 1024

TM_QKV = 512
TN_QKV = 256
TM_OPROJ = 512
TQ_FLASH = 2048
TK_FLASH = 1024
FLASH_MARGIN = 60.0
LOG2E = math.log2(math.e)
TQ_WIN = 512
TM_MOE = 256
TMX = 512
TD_F = 512
N2 = 128
NEG = -1e30


def _cp(sem, vmem=VMEM_LIMIT):
    return pltpu.CompilerParams(dimension_semantics=sem, vmem_limit_bytes=vmem)


def _dot(a, b):
    return jnp.dot(a, b, preferred_element_type=F32)


def _dot_nt(a, b):
    return lax.dot_general(a, b, (((1,), (1,)), ((), ())), preferred_element_type=F32)


def _rms(x):
    return x * lax.rsqrt(jnp.mean(x * x, axis=-1, keepdims=True) + EPS)


def _mod_kernel(c_ref, w_ref, b_ref, o_ref):
    a = c_ref[...]
    a = a * jax.nn.sigmoid(a)
    o_ref[...] = _dot(a.astype(BF16), w_ref[...].astype(BF16)) + b_ref[...]


def _modulation(cvec, w_mod, b_mod):
    depth, _, nm = w_mod.shape
    tn = 1024
    return pl.pallas_call(
        _mod_kernel,
        grid=(depth, nm // tn),
        in_specs=[
            pl.BlockSpec((SUB, D), lambda l, j: (0, 0)),
            pl.BlockSpec((None, D, tn), lambda l, j: (l, 0, j)),
            pl.BlockSpec((None, 1, tn), lambda l, j: (l, 0, j)),
        ],
        out_specs=pl.BlockSpec((None, SUB, tn), lambda l, j: (l, 0, j)),
        out_shape=jax.ShapeDtypeStruct((depth, SUB, nm), F32),
        compiler_params=_cp(("arbitrary", "arbitrary")),
        name="modulation",
    )(cvec, w_mod, b_mod.reshape(depth, 1, nm))


def _qkv_kernel(x_ref, g_ref, mod_ref, w_ref, hg_ref, cos_ref, sin_ref, o_ref, *, do_norm, n_rot):
    h = _rms(x_ref[...]) * g_ref[...]
    h = (h * (1.0 + mod_ref[1:2, :]) + mod_ref[0:1, :]).astype(BF16)
    cos = cos_ref[...]
    sin = sin_ref[...]
    lane = lax.broadcasted_iota(jnp.int32, cos.shape, 1)
    first = (lane % (2 * ROT_FREQS)) < ROT_FREQS
    for j in range(QKV // TN_QKV):
        cols = slice(j * TN_QKV, (j + 1) * TN_QKV)
        y = _dot(h, w_ref[:, cols])
        if j >= n_rot:
            o_ref[:, cols] = y.astype(BF16)
            continue
        outs = []
        for hh in range(TN_QKV // HD):
            yh = y[:, hh * HD:(hh + 1) * HD]
            if do_norm:
                yh = _rms(yh)
            yh = yh * hg_ref[:, j * TN_QKV + hh * HD:j * TN_QKV + (hh + 1) * HD]
            partner = jnp.where(first, pltpu.roll(yh, HD - ROT_FREQS, 1), pltpu.roll(yh, ROT_FREQS, 1))
            outs.append(yh * cos + partner * sin)
        o_ref[:, cols] = jnp.concatenate(outs, axis=1).astype(BF16)


def _qkv_proj(x_all, gain, mod, w_bf, head_gain, cos_e, sin_e, tpb, nb, do_norm):
    r = x_all.shape[0]
    n_rot = (NH + NKV) * HD // TN_QKV
    return pl.pallas_call(
        functools.partial(_qkv_kernel, do_norm=do_norm, n_rot=n_rot),
        grid=(r // TM_QKV,),
        in_specs=[
            pl.BlockSpec((TM_QKV, D), lambda i: (i, 0)),
            pl.BlockSpec((1, D), lambda i: (0, 0)),
            pl.BlockSpec((None, NMOD, D), lambda i: (jnp.minimum(i // tpb, nb), 0, 0)),
            pl.BlockSpec((D, QKV), lambda i: (0, 0)),
            pl.BlockSpec((1, QKV), lambda i: (0, 0)),
            pl.BlockSpec((TM_QKV, HD), lambda i: (i, 0)),
            pl.BlockSpec((TM_QKV, HD), lambda i: (i, 0)),
        ],
        out_specs=pl.BlockSpec((TM_QKV, QKV), lambda i: (i, 0)),
        out_shape=jax.ShapeDtypeStruct((r, QKV), BF16),
        compiler_params=_cp(("parallel",)),
        name="qkv_proj",
    )(x_all, gain, mod, w_bf, head_gain, cos_e, sin_e)


def _flash_kernel(q_ref, kc_ref, vc_ref, k_ref, v_ref, o_ref, m_scr, acc_scr):
    j = pl.program_id(3)

    def chunks(s):
        return [s[:, c * LANES:(c + 1) * LANES] for c in range(s.shape[1] // LANES)]

    def rebase_update(k, v1):
        for g in range(GQ):
            ch = chunks(_dot_nt(q_ref[:, g * HD:(g + 1) * HD], k))
            m_prev = m_scr[g]
            m_new = jnp.maximum(m_prev, jnp.max(functools.reduce(jnp.maximum, ch), axis=-1, keepdims=True))
            alpha = jnp.exp2(m_prev - m_new)
            p = jnp.concatenate([jnp.exp2(c - m_new).astype(BF16) for c in ch], axis=1)
            acc = acc_scr[g]
            acc_scr[g] = jnp.concatenate([acc[:, :HD] * alpha, acc[:, HD:] * alpha], axis=1) + _dot(p, v1)
            m_scr[g] = m_new

    @pl.when(j == 0)
    def _():
        m_scr[...] = jnp.full(m_scr.shape, NEG, F32)
        acc_scr[...] = jnp.zeros(acc_scr.shape, F32)
        vc = vc_ref[...]
        rebase_update(kc_ref[...], jnp.concatenate([vc, jnp.ones_like(vc)], axis=1))

    k = k_ref[...]
    v = v_ref[...]
    v1 = jnp.concatenate([v, jnp.ones_like(v)], axis=1)
    ps = []
    excess = None
    for g in range(GQ):
        ch = chunks(_dot_nt(q_ref[:, g * HD:(g + 1) * HD], k))
        m = m_scr[g]
        over = functools.reduce(jnp.maximum, ch) - m
        excess = over if excess is None else jnp.maximum(excess, over)
        ps.append(jnp.concatenate([jnp.exp2(c - m).astype(BF16) for c in ch], axis=1))
    worst = jnp.max(jnp.max(excess, axis=0, keepdims=True), axis=1, keepdims=True)[0, 0]

    @pl.when(worst <= FLASH_MARGIN)
    def _():
        for g in range(GQ):
            acc_scr[g] += _dot(ps[g], v1)

    @pl.when(worst > FLASH_MARGIN)
    def _():
        rebase_update(k, v1)

    @pl.when(j == pl.num_programs(3) - 1)
    def _():
        for g in range(GQ):
            acc = acc_scr[g]
            o_ref[:, g * HD:(g + 1) * HD] = (acc[:, :HD] / acc[:, HD:]).astype(BF16)


def _global_attention(qkv, nb, s, c):
    nq = s // TQ_FLASH
    nk = s // TK_FLASH
    kcol = NH
    vcol = NH + NKV
    cblk = nb * s // c
    return pl.pallas_call(
        _flash_kernel,
        grid=(nb, NKV, nq, nk),
        in_specs=[
            pl.BlockSpec((TQ_FLASH, GQ * HD), lambda b, h, i, j: (b * nq + i, h), pipeline_mode=pl.Buffered(1)),
            pl.BlockSpec((c, HD), lambda b, h, i, j: (cblk + b, kcol + h)),
            pl.BlockSpec((c, HD), lambda b, h, i, j: (cblk + b, vcol + h)),
            pl.BlockSpec((TK_FLASH, HD), lambda b, h, i, j: (b * nk + j, kcol + h)),
            pl.BlockSpec((TK_FLASH, HD), lambda b, h, i, j: (b * nk + j, vcol + h)),
        ],
        out_specs=pl.BlockSpec((TQ_FLASH, GQ * HD), lambda b, h, i, j: (b * nq + i, h), pipeline_mode=pl.Buffered(1)),
        out_shape=jax.ShapeDtypeStruct((nb * s, NH * HD), BF16),
        scratch_shapes=[
            pltpu.VMEM((GQ, TQ_FLASH, LANES), F32),
            pltpu.VMEM((GQ, TQ_FLASH, 2 * HD), F32),
        ],
        compiler_params=_cp(("parallel", "parallel", "parallel", "arbitrary"), vmem=VMEM_LIMIT_FLASH),
        name="global_attention",
    )(qkv, qkv, qkv, qkv, qkv)


def _window_kernel(sink_ref, q_ref, kc_ref, vc_ref, kp_ref, vp_ref, k0_ref, v0_ref, kn_ref, vn_ref, bias_ref, o_ref):
    h = pl.program_id(1)
    k = jnp.concatenate([kc_ref[...], kp_ref[...], k0_ref[...], kn_ref[...]], axis=0)
    v = jnp.concatenate([vc_ref[...], vp_ref[...], v0_ref[...], vn_ref[...]], axis=0)
    v1 = jnp.concatenate([v, jnp.ones_like(v)], axis=1)
    bias = bias_ref[...]
    for g in range(GQ):
        sink = sink_ref[h * GQ + g] * LOG2E
        s = _dot_nt(q_ref[:, g * HD:(g + 1) * HD], k) + bias
        m = jnp.maximum(jnp.max(s, axis=-1, keepdims=True), sink)
        acc = _dot(jnp.exp2(s - m).astype(BF16), v1)
        o_ref[:, g * HD:(g + 1) * HD] = (acc[:, :HD] / (acc[:, HD:] + jnp.exp2(sink - m))).astype(BF16)


def _window_bias(c):
    q = jnp.arange(TQ_WIN, dtype=jnp.int32)[:, None]
    kk = jnp.arange(WINDOW, dtype=jnp.int32)[None, :]
    before = kk >= q
    after = q >= TQ_WIN - WINDOW + kk
    inside = jnp.abs(q - jnp.arange(TQ_WIN, dtype=jnp.int32)[None, :]) <= WINDOW
    ctx = jnp.ones((TQ_WIN, c), bool)
    variants = []
    for var in range(4):
        ok = jnp.concatenate([ctx, before & ((var & 1) == 0), inside, after & ((var & 2) == 0)], axis=1)
        variants.append(jnp.where(ok, 0.0, NEG).astype(F32))
    return jnp.stack(variants)


def _window_attention(qkv, sink, nb, s, c):
    tq = TQ_WIN
    nq = s // tq
    per = tq // WINDOW
    nw = s // WINDOW
    kcol = NH
    vcol = NH + NKV
    cblk = nb * s // c

    def edge(col, off):
        return pl.BlockSpec((WINDOW, HD), lambda b, h, i: (b * nw + jnp.clip(i * per + off, 0, nw - 1), col + h))

    def tile(col):
        return pl.BlockSpec((tq, HD), lambda b, h, i: (b * nq + i, col + h))

    return pl.pallas_call(
        _window_kernel,
        grid=(nb, NKV, nq),
        in_specs=[
            pl.BlockSpec(memory_space=pltpu.SMEM),
            pl.BlockSpec((tq, GQ * HD), lambda b, h, i: (b * nq + i, h)),
            pl.BlockSpec((c, HD), lambda b, h, i: (cblk + b, kcol + h)),
            pl.BlockSpec((c, HD), lambda b, h, i: (cblk + b, vcol + h)),
            edge(kcol, -1), edge(vcol, -1), tile(kcol), tile(vcol), edge(kcol, per), edge(vcol, per),
            pl.BlockSpec((None, tq, c + 2 * WINDOW + tq),
                         lambda b, h, i: ((i == 0).astype(jnp.int32) + 2 * (i == nq - 1).astype(jnp.int32), 0, 0)),
        ],
        out_specs=pl.BlockSpec((tq, GQ * HD), lambda b, h, i: (b * nq + i, h)),
        out_shape=jax.ShapeDtypeStruct((nb * s, NH * HD), BF16),
        compiler_params=_cp(("parallel", "parallel", "parallel")),
        name="window_attention",
    )(sink, qkv, qkv, qkv, qkv, qkv, qkv, qkv, qkv, qkv, _window_bias(c))


def _ctx_attn_kernel(sink_ref, q_ref, k_ref, v_ref, o_ref, *, use_sink):
    h = pl.program_id(1)
    k = k_ref[...]
    v = v_ref[...]
    for g in range(GQ):
        s = _dot_nt(q_ref[:, g * HD:(g + 1) * HD], k)
        m = jnp.max(s, axis=-1, keepdims=True)
        if use_sink:
            sink = sink_ref[h * GQ + g] * LOG2E
            m = jnp.maximum(m, sink)
        p = jnp.exp2(s - m)
        l = jnp.sum(p, axis=-1, keepdims=True)
        if use_sink:
            l = l + jnp.exp2(sink - m)
        o_ref[:, g * HD:(g + 1) * HD] = (_dot(p.astype(BF16), v) / l).astype(BF16)


def _ctx_attention(qkv, sink, nb, s, c, use_sink):
    cblk = nb * s // c
    return pl.pallas_call(
        functools.partial(_ctx_attn_kernel, use_sink=use_sink),
        grid=(nb, NKV),
        in_specs=[
            pl.BlockSpec(memory_space=pltpu.SMEM),
            pl.BlockSpec((c, GQ * HD), lambda b, h: (cblk + b, h)),
            pl.BlockSpec((c, HD), lambda b, h: (cblk + b, NH + h)),
            pl.BlockSpec((c, HD), lambda b, h: (cblk + b, NH + NKV + h)),
        ],
        out_specs=pl.BlockSpec((c, GQ * HD), lambda b, h: (b, h)),
        out_shape=jax.ShapeDtypeStruct((nb * c, NH * HD), BF16),
        compiler_params=_cp(("arbitrary", "arbitrary")),
        name="ctx_attention",
    )(sink, qkv, qkv, qkv)


def _oproj_kernel(a_ref, ac_ref, w_ref, x_ref, g_ref, mod_ref, o_ref, *, nlat):
    def project(src_ref):
        y = _dot(src_ref[...].astype(BF16), w_ref[...])
        o_ref[...] = x_ref[...] + mod_ref[2:3, :] * (_rms(y) * g_ref[...])

    i = pl.program_id(0)

    @pl.when(i < nlat)
    def _():
        project(a_ref)

    @pl.when(i >= nlat)
    def _():
        project(ac_ref)


def _oproj(a_lat, a_ctx, w_bf, x_all, gain, mod, tpb, nb, with_ctx):
    tm = TM_OPROJ
    nlat = a_lat.shape[0] // tm
    ntiles = nlat + (a_ctx.shape[0] // tm if with_ctx else 0)
    return pl.pallas_call(
        functools.partial(_oproj_kernel, nlat=nlat),
        grid=(ntiles,),
        in_specs=[
            pl.BlockSpec((tm, D), lambda i: (jnp.minimum(i, nlat - 1), 0)),
            pl.BlockSpec((tm, D), lambda i: (jnp.maximum(i - nlat, 0), 0)),
            pl.BlockSpec((D, D), lambda i: (0, 0)),
            pl.BlockSpec((tm, D), lambda i: (i, 0)),
            pl.BlockSpec((1, D), lambda i: (0, 0)),
            pl.BlockSpec((None, NMOD, D), lambda i: (jnp.minimum(i // tpb, nb), 0, 0)),
        ],
        out_specs=pl.BlockSpec((tm, D), lambda i: (i, 0)),
        out_shape=jax.ShapeDtypeStruct(x_all.shape, F32),
        input_output_aliases={3: 0},
        compiler_params=_cp(("parallel",)),
        name="oproj",
    )(a_lat, a_ctx, w_bf, x_all, gain, mod)


def _dft_mats(n):
    k = np.arange(n)
    ang = 2.0 * np.pi * ((k[:, None] * k[None, :]) % n) / n
    return np.cos(ang) / math.sqrt(n), np.sin(ang) / math.sqrt(n)


def _stage1_matrix(n1):
    c, s = _dft_mats(n1)
    eye = np.eye(SUB)
    return np.concatenate([np.kron(c, eye), -np.kron(s, eye)], axis=0)


def _stage2_matrix():
    c, s = _dft_mats(N2)
    return np.block([[c, s], [-s, c]])


def _channel_matrix():
    c, s = _dft_mats(FC)
    return np.concatenate([c, s], axis=0)


def _fft1_kernel(x_ref, g_ref, mod_ref, kr_ref, tc_ref, ts_ref, ar_ref, ai_ref, rstd_scr):
    dt = pl.program_id(2)
    n1 = x_ref.shape[0]
    rows = n1 * SUB

    @pl.when(dt == 0)
    def _():
        x = x_ref[...].reshape(rows, D)
        rstd_scr[...] = lax.rsqrt(jnp.mean(x * x, axis=-1, keepdims=True) + EPS)

    col = pl.multiple_of(dt * TD_F, TD_F)
    x = x_ref[:, :, pl.ds(col, TD_F)].reshape(rows, TD_F)
    h = x * rstd_scr[...] * g_ref[...]
    h = h * (1.0 + mod_ref[1:2, :]) + mod_ref[0:1, :]
    y = _dot(kr_ref[...], h.astype(BF16))
    yr = y[:rows]
    yi = y[rows:]
    tc = tc_ref[...]
    ts = ts_ref[...]
    for cc in range(TD_F // LANES):
        sl = slice(cc * LANES, (cc + 1) * LANES)
        a = yr[:, sl]
        b = yi[:, sl]
        ar_ref[:, :, sl] = (a * tc + b * ts).reshape(n1, SUB, LANES)
        ai_ref[:, :, sl] = (b * tc - a * ts).reshape(n1, SUB, LANES)


def _fft2_kernel(ar_ref, ai_ref, w2_ref, cs_ref, o_ref):
    yr, yi = [], []
    for jj in range(SUB):
        rows = slice(jj * N2, (jj + 1) * N2)
        a = jnp.concatenate([ar_ref[rows, :], ai_ref[rows, :]], axis=0).astype(BF16)
        y = _dot(w2_ref[...], a)
        yr.append(y[:N2])
        yi.append(y[N2:])
    yr = jnp.concatenate(yr, axis=0).astype(BF16)
    yi = jnp.concatenate(yi, axis=0).astype(BF16)
    for gi in range(TD_F // FC):
        sl = slice(gi * FC, (gi + 1) * FC)
        res = _dot(jnp.concatenate([yr[:, sl], yi[:, sl]], axis=1), cs_ref[...])
        for jj in range(SUB):
            o_ref[:, jj, sl] = res[jj * N2:(jj + 1) * N2]


def _fnet_ctx_kernel(x_ref, g_ref, mod_ref, f_ref, cs_ref, o_ref):
    c = x_ref.shape[0]
    h = _rms(x_ref[...]) * g_ref[...]
    h = h * (1.0 + mod_ref[1:2, :]) + mod_ref[0:1, :]
    y = _dot(f_ref[...], h.astype(BF16))
    yr = y[:c].astype(BF16)
    yi = y[c:].astype(BF16)
    for gi in range(FG):
        sl = slice(gi * FC, (gi + 1) * FC)
        z = jnp.concatenate([yr[:, sl], yi[:, sl]], axis=1)
        o_ref[:, sl] = _dot(z, cs_ref[...])


def _fourier_mix(x_all, gain, mod, nb, s, c, with_ctx):
    r = x_all.shape[0]
    n1 = s // N2
    nch = N2 // SUB
    ndt = D // TD_F
    kr = jnp.asarray(_stage1_matrix(n1), BF16)
    k2 = jnp.asarray(_stage2_matrix(), BF16)
    cs = jnp.asarray(_channel_matrix(), BF16)
    rows = n1 * SUB
    k1 = jnp.arange(n1, dtype=jnp.int32)[None, :, None]
    n2 = (jnp.arange(nch, dtype=jnp.int32)[:, None, None] * SUB + jnp.arange(SUB, dtype=jnp.int32)[None, None, :])
    ang = (2.0 * np.pi / s) * ((k1 * n2) % s).astype(F32).reshape(nch, rows, 1)
    tc = jnp.broadcast_to(jnp.cos(ang), (nch, rows, LANES))
    ts = jnp.broadcast_to(jnp.sin(ang), (nch, rows, LANES))

    x4 = x_all.reshape(r // N2, N2, D)
    ar, ai = pl.pallas_call(
        _fft1_kernel,
        grid=(nb, nch, ndt),
        in_specs=[
            pl.BlockSpec((n1, SUB, D), lambda b, ch, dt: (b, ch, 0)),
            pl.BlockSpec((1, TD_F), lambda b, ch, dt: (0, dt)),
            pl.BlockSpec((None, NMOD, TD_F), lambda b, ch, dt: (b, 0, dt)),
            pl.BlockSpec((2 * rows, rows), lambda b, ch, dt: (0, 0)),
            pl.BlockSpec((None, rows, LANES), lambda b, ch, dt: (ch, 0, 0)),
            pl.BlockSpec((None, rows, LANES), lambda b, ch, dt: (ch, 0, 0)),
        ],
        out_specs=[
            pl.BlockSpec((n1, SUB, TD_F), lambda b, ch, dt: (b, ch, dt)),
            pl.BlockSpec((n1, SUB, TD_F), lambda b, ch, dt: (b, ch, dt)),
        ],
        out_shape=[jax.ShapeDtypeStruct((nb * n1, N2, D), F32)] * 2,
        scratch_shapes=[pltpu.VMEM((rows, 1), F32)],
        compiler_params=_cp(("parallel", "parallel", "arbitrary")),
        name="fnet_stage1",
    )(x4, gain, mod, kr, tc, ts)

    ar2 = ar.reshape(nb * s, D)
    ai2 = ai.reshape(nb * s, D)
    nkc = n1 // SUB
    mixed = pl.pallas_call(
        _fft2_kernel,
        grid=(nb, nkc, ndt),
        in_specs=[
            pl.BlockSpec((N2 * SUB, TD_F), lambda b, kc, dt: (b * nkc + kc, dt)),
            pl.BlockSpec((N2 * SUB, TD_F), lambda b, kc, dt: (b * nkc + kc, dt)),
            pl.BlockSpec((2 * N2, 2 * N2), lambda b, kc, dt: (0, 0)),
            pl.BlockSpec((2 * FC, FC), lambda b, kc, dt: (0, 0)),
        ],
        out_specs=pl.BlockSpec((N2, SUB, TD_F), lambda b, kc, dt: (b, kc, dt)),
        out_shape=jax.ShapeDtypeStruct((nb * s // n1, n1, D), F32),
        compiler_params=_cp(("parallel", "parallel", "parallel")),
        name="fnet_stage2",
    )(ar2, ai2, k2, cs)
    mixed = mixed.reshape(nb * s, D)

    mixed_ctx = None
    if with_ctx:
        fc, fs = _dft_mats(c)
        fmat = jnp.asarray(np.concatenate([fc, -fs], axis=0), BF16)
        cblk = nb * s // c
        mixed_ctx = pl.pallas_call(
            _fnet_ctx_kernel,
            grid=(nb,),
            in_specs=[
                pl.BlockSpec((c, D), lambda b: (cblk + b, 0)),
                pl.BlockSpec((1, D), lambda b: (0, 0)),
                pl.BlockSpec((None, NMOD, D), lambda b: (nb, 0, 0)),
                pl.BlockSpec((2 * c, c), lambda b: (0, 0)),
                pl.BlockSpec((2 * FC, FC), lambda b: (0, 0)),
            ],
            out_specs=pl.BlockSpec((c, D), lambda b: (b, 0)),
            out_shape=jax.ShapeDtypeStruct((nb * c, D), F32),
            compiler_params=_cp(("arbitrary",)),
            name="fnet_ctx",
        )(x_all, gain, mod, fmat, cs)
    return mixed, mixed_ctx


M_E1, M_E2, M_R1, M_R2, M_W1, M_W2, M_PG = range(7)
LANE_E0 = NG


def _ffn_tokens(x_ref, g_ref, mod_ref):
    tok = _rms(x_ref[...]) * g_ref[...]
    return tok * (1.0 + mod_ref[4:5, :]) + mod_ref[3:4, :]


def _route_kernel(x_ref, g_ref, mod_ref, wr_ref, br_ref, meta_ref, cnt_ref, base_scr):
    i = pl.program_id(0)
    tm = x_ref.shape[0]

    @pl.when(i == 0)
    def _():
        base_scr[...] = jnp.zeros(base_scr.shape, F32)

    tok = _ffn_tokens(x_ref, g_ref, mod_ref)
    wr = wr_ref[...]
    w_hi = wr.astype(BF16)
    w_lo = (wr - w_hi.astype(F32)).astype(BF16)
    t_hi = tok.astype(BF16)
    t_lo = (tok - t_hi.astype(F32)).astype(BF16)
    logits = _dot(t_hi, w_hi) + (_dot(t_hi, w_lo) + _dot(t_lo, w_hi)) + br_ref[...]

    lane = lax.broadcasted_iota(jnp.int32, logits.shape, 1)
    big = jnp.int32(LANES)
    gmask = lane < NG
    gl = jnp.where(gmask, logits, NEG)
    gmax = jnp.max(gl, axis=-1, keepdims=True)
    gsum = jnp.sum(jnp.where(gmask, jnp.exp(gl - gmax), 0.0), axis=-1, keepdims=True)
    gidx = jnp.min(jnp.where(gl == gmax, lane, big), axis=-1, keepdims=True)
    lo = LANE_E0 + gidx * EPG
    emask = (lane >= lo) & (lane < lo + EPG)
    el = jnp.where(emask, logits, NEG)
    emax = jnp.max(el, axis=-1, keepdims=True)
    idx1 = jnp.min(jnp.where(el == emax, lane, big), axis=-1, keepdims=True)
    el2 = jnp.where(lane == idx1, NEG, el)
    e2max = jnp.max(el2, axis=-1, keepdims=True)
    idx2 = jnp.min(jnp.where(el2 == e2max, lane, big), axis=-1, keepdims=True)
    t = jnp.exp(e2max - emax)
    w1 = 1.0 / (1.0 + t)
    w2 = t / (1.0 + t)

    oh1 = (lane == idx1).astype(F32)
    oh2 = (lane == idx2).astype(F32)
    rr = lax.broadcasted_iota(jnp.int32, (tm, tm), 0)
    cc = lax.broadcasted_iota(jnp.int32, (tm, tm), 1)
    tri = (cc < rr).astype(BF16)
    base = base_scr[0:1, :]
    cnt1 = jnp.sum(oh1, axis=0, keepdims=True)
    cnt2 = jnp.sum(oh2, axis=0, keepdims=True)
    pre1 = _dot(tri, oh1.astype(BF16)) + base
    pre2 = _dot(tri, oh2.astype(BF16)) + base + cnt1
    r1 = jnp.sum(oh1 * pre1, axis=-1, keepdims=True)
    r2 = jnp.sum(oh2 * pre2, axis=-1, keepdims=True)
    new_base = base + cnt1 + cnt2
    base_scr[0:1, :] = new_base
    cnt_ref[...] = jnp.broadcast_to(new_base, cnt_ref.shape)

    rec = jnp.zeros(logits.shape, F32)
    for k, val in ((M_E1, (idx1 - LANE_E0).astype(F32)), (M_E2, (idx2 - LANE_E0).astype(F32)),
                   (M_R1, r1), (M_R2, r2), (M_W1, w1), (M_W2, w2), (M_PG, 1.0 / gsum)):
        rec = jnp.where(lane == k, val, rec)
    meta_ref[...] = rec


def _pack_pairs(t):
    half = t.shape[1] // 2
    hi = lax.bitcast_convert_type(t[:, :half].astype(BF16).astype(F32), U32)
    lo = lax.bitcast_convert_type(t[:, half:].astype(BF16).astype(F32), U32)
    return hi | (lo >> 16)


def _unpack_pairs(w):
    hi = lax.bitcast_convert_type(w & jnp.uint32(0xFFFF0000), F32)
    lo = lax.bitcast_convert_type(w << 16, F32)
    return jnp.concatenate([hi, lo], axis=1)


def _row_copy_wait(hbm_ref, vmem_ref, sem):
    pltpu.make_async_copy(hbm_ref.at[pl.ds(0, vmem_ref.shape[0])], vmem_ref, sem).wait()


def _dispatch_kernel(pt_ref, d_ref, x_ref, g_ref, mod_ref, xs_hbm, buf, zbuf, sem, zsem):
    i = pl.program_id(0)
    nt = pl.num_programs(0)
    tm = x_ref.shape[0]
    slot = i % 2

    @pl.when(i == 0)
    def _():
        zbuf[...] = jnp.zeros(zbuf.shape, U32)

        def zero_tile(t, carry):
            @pl.when(pt_ref[t] != 0)
            def _():
                pltpu.make_async_copy(zbuf, xs_hbm.at[pl.ds(pl.multiple_of(t * TMX, TMX), TMX)], zsem).start()
            return carry

        def zero_wait(t, carry):
            @pl.when(pt_ref[t] != 0)
            def _():
                _row_copy_wait(xs_hbm, zbuf, zsem)
            return carry

        lax.fori_loop(0, pt_ref.shape[0], zero_tile, 0)
        lax.fori_loop(0, pt_ref.shape[0], zero_wait, 0)

    @pl.when(i >= 2)
    def _():
        _row_copy_wait(xs_hbm, buf.at[slot], sem.at[slot])
        _row_copy_wait(xs_hbm, buf.at[slot], sem.at[slot])

    buf[slot] = _pack_pairs(_ffn_tokens(x_ref, g_ref, mod_ref))

    def issue(rw, carry):
        src = buf.at[slot, pl.ds(rw, 1)]
        pltpu.make_async_copy(src, xs_hbm.at[pl.ds(d_ref[0, rw], 1)], sem.at[slot]).start()
        pltpu.make_async_copy(src, xs_hbm.at[pl.ds(d_ref[0, tm + rw], 1)], sem.at[slot]).start()
        return carry

    for rw in range(tm):
        issue(rw, 0)

    @pl.when(i == nt - 1)
    def _():
        _row_copy_wait(xs_hbm, buf.at[slot], sem.at[slot])
        _row_copy_wait(xs_hbm, buf.at[slot], sem.at[slot])

        @pl.when(nt >= 2)
        def _():
            _row_copy_wait(xs_hbm, buf.at[1 - slot], sem.at[1 - slot])
            _row_copy_wait(xs_hbm, buf.at[1 - slot], sem.at[1 - slot])


def _expert_kernel(te_ref, nv_ref, xs_ref, wg_ref, wu_ref, wd_ref, o_ref, wg_s, wu_s, wd_s):
    t = pl.program_id(0)

    @pl.when(t < nv_ref[0])
    def _():
        prev = te_ref[jnp.maximum(t - 1, 0)]

        @pl.when((t == 0) | (te_ref[t] != prev))
        def _():
            wg_s[...] = wg_ref[...].astype(BF16)
            wu_s[...] = wu_ref[...].astype(BF16)
            wd_s[...] = wd_ref[...].astype(BF16)

        x = _unpack_pairs(xs_ref[...]).astype(BF16)
        a = _dot(x, wg_s[...])
        u = _dot(x, wu_s[...])
        hmid = (a * jax.nn.sigmoid(a) * u).astype(BF16)
        o_ref[...] = _pack_pairs(_dot(hmid, wd_s[...]))

    @pl.when(t >= nv_ref[0])
    def _():
        o_ref[...] = jnp.zeros(o_ref.shape, U32)


def _combine_kernel(d_ref, dn_ref, meta_ref, ys_hbm, x_ref, g_ref, mod_ref, o_ref, buf, sem):
    i = pl.program_id(0)
    nt = pl.num_programs(0)
    tm = x_ref.shape[0]
    slot = i % 2

    def gather(idx_ref, sl):
        def issue(rw, carry):
            pltpu.make_async_copy(ys_hbm.at[pl.ds(idx_ref[0, rw], 1)], buf.at[sl, 0, pl.ds(rw, 1)], sem.at[sl]).start()
            pltpu.make_async_copy(ys_hbm.at[pl.ds(idx_ref[0, tm + rw], 1)], buf.at[sl, 1, pl.ds(rw, 1)], sem.at[sl]).start()
            return carry

        for rw in range(tm):
            issue(rw, 0)

    @pl.when(i == 0)
    def _():
        gather(d_ref, 0)

    @pl.when(i + 1 < nt)
    def _():
        gather(dn_ref, 1 - slot)

    _row_copy_wait(ys_hbm, buf.at[slot, 0], sem.at[slot])
    _row_copy_wait(ys_hbm, buf.at[slot, 1], sem.at[slot])
    meta = meta_ref[...]
    w1 = meta[:, M_W1:M_W1 + 1]
    w2 = meta[:, M_W2:M_W2 + 1]
    pg = meta[:, M_PG:M_PG + 1]
    y = w1 * _unpack_pairs(buf[slot, 0]) + w2 * _unpack_pairs(buf[slot, 1])
    o_ref[...] = x_ref[...] + mod_ref[5:6, :] * (pg * (_rms(y) * g_ref[...]))


def _moe(x_all, layer, pre_gain, post_gain, mod, wr, br, w_gate, w_up, w_down, tpb, nb, nrows, out_rows):
    tm = TM_MOE
    nt = nrows // tm
    mod_map = lambda i: (jnp.minimum(i // tpb, nb), 0, 0)
    meta, cnt = pl.pallas_call(
        _route_kernel,
        grid=(nt,),
        in_specs=[
            pl.BlockSpec((tm, D), lambda i: (i, 0)),
            pl.BlockSpec((1, D), lambda i: (0, 0)),
            pl.BlockSpec((None, NMOD, D), mod_map),
            pl.BlockSpec((D, LANES), lambda i: (0, 0)),
            pl.BlockSpec((1, LANES), lambda i: (0, 0)),
        ],
        out_specs=[
            pl.BlockSpec((tm, LANES), lambda i: (i, 0)),
            pl.BlockSpec((SUB, LANES), lambda i: (0, 0)),
        ],
        out_shape=[
            jax.ShapeDtypeStruct((nrows, LANES), F32),
            jax.ShapeDtypeStruct((SUB, LANES), F32),
        ],
        scratch_shapes=[pltpu.VMEM((SUB, LANES), F32)],
        compiler_params=_cp(("arbitrary",)),
        name="moe_route",
    )(x_all, pre_gain, mod, wr, br)

    ntile = nrows * 2 // TMX + NE
    nslot = ntile * TMX
    e = meta[:, M_E1:M_E2 + 1].astype(jnp.int32)
    rk = meta[:, M_R1:M_R2 + 1].astype(jnp.int32)
    counts = cnt[0, LANE_E0:LANE_E0 + NE].astype(jnp.int32)
    padded = (counts + TMX - 1) // TMX * TMX
    seg_end = jnp.cumsum(padded)
    seg_start = seg_end - padded
    onehot = e[:, :, None] == jnp.arange(NE, dtype=jnp.int32)[None, None, :]
    dest = jnp.sum(jnp.where(onehot, seg_start[None, None, :], 0), axis=-1) + rk
    dtile = dest.reshape(nt, tm, 2).transpose(0, 2, 1).reshape(nt, 1, 2 * tm)
    nvalid = (seg_end[-1] // TMX).astype(jnp.int32).reshape(1)
    tile_id = jnp.arange(ntile, dtype=jnp.int32)
    te_raw = jnp.sum((tile_id[:, None] * TMX >= seg_end[None, :]).astype(jnp.int32), axis=1)
    te = jnp.minimum(te_raw, te_raw[jnp.maximum(nvalid[0] - 1, 0)])
    last_of_seg = jnp.concatenate([te_raw[1:] != te_raw[:-1], jnp.ones((1,), bool)])
    partial = (last_of_seg | (tile_id >= nvalid[0])).astype(jnp.int32)

    xs = pl.pallas_call(
        _dispatch_kernel,
        grid_spec=pltpu.PrefetchScalarGridSpec(
            num_scalar_prefetch=1,
            grid=(nt,),
            in_specs=[
                pl.BlockSpec((None, 1, 2 * tm), lambda i, pt: (i, 0, 0), memory_space=pltpu.SMEM),
                pl.BlockSpec((tm, D), lambda i, pt: (i, 0)),
                pl.BlockSpec((1, D), lambda i, pt: (0, 0)),
                pl.BlockSpec((None, NMOD, D), lambda i, pt: mod_map(i)),
            ],
            out_specs=pl.BlockSpec(memory_space=pl.ANY),
            scratch_shapes=[
                pltpu.VMEM((2, tm, DP), U32),
                pltpu.VMEM((TMX, DP), U32),
                pltpu.SemaphoreType.DMA((2,)),
                pltpu.SemaphoreType.DMA(()),
            ],
        ),
        out_shape=jax.ShapeDtypeStruct((nslot, DP), U32),
        compiler_params=_cp(("arbitrary",)),
        name="moe_dispatch",
    )(partial, dtile, x_all, pre_gain, mod)

    ys = pl.pallas_call(
        _expert_kernel,
        grid_spec=pltpu.PrefetchScalarGridSpec(
            num_scalar_prefetch=2,
            grid=(ntile,),
            in_specs=[
                pl.BlockSpec((TMX, DP), lambda t, te_r, nv_r: (jnp.minimum(t, nv_r[0] - 1), 0)),
                pl.BlockSpec((None, None, D, FF), lambda t, te_r, nv_r: (layer, te_r[t], 0, 0)),
                pl.BlockSpec((None, None, D, FF), lambda t, te_r, nv_r: (layer, te_r[t], 0, 0)),
                pl.BlockSpec((None, None, FF, D), lambda t, te_r, nv_r: (layer, te_r[t], 0, 0)),
            ],
            out_specs=pl.BlockSpec((TMX, DP), lambda t, te_r, nv_r: (t, 0)),
            scratch_shapes=[
                pltpu.VMEM((D, FF), BF16),
                pltpu.VMEM((D, FF), BF16),
                pltpu.VMEM((FF, D), BF16),
            ],
        ),
        out_shape=jax.ShapeDtypeStruct((nslot, DP), U32),
        compiler_params=_cp(("arbitrary",)),
        name="moe_experts",
    )(te, nvalid, xs, w_gate, w_up, w_down)

    alias = {} if out_rows != x_all.shape[0] else {4: 0}
    return pl.pallas_call(
        _combine_kernel,
        grid=(nt,),
        in_specs=[
            pl.BlockSpec((None, 1, 2 * tm), lambda i: (i, 0, 0), memory_space=pltpu.SMEM),
            pl.BlockSpec((None, 1, 2 * tm), lambda i: (jnp.minimum(i + 1, nt - 1), 0, 0), memory_space=pltpu.SMEM),
            pl.BlockSpec((tm, LANES), lambda i: (i, 0)),
            pl.BlockSpec(memory_space=pl.ANY),
            pl.BlockSpec((tm, D), lambda i: (i, 0)),
            pl.BlockSpec((1, D), lambda i: (0, 0)),
            pl.BlockSpec((None, NMOD, D), mod_map),
        ],
        out_specs=pl.BlockSpec((tm, D), lambda i: (i, 0)),
        out_shape=jax.ShapeDtypeStruct((out_rows, D), F32),
        scratch_shapes=[pltpu.VMEM((2, 2, tm, DP), U32), pltpu.SemaphoreType.DMA((2,))],
        input_output_aliases=alias,
        compiler_params=_cp(("arbitrary",)),
        name="moe_combine",
    )(dtile, dtile, meta, ys, x_all, post_gain, mod)


def _rope_tables(nb, s, c):
    pos = jnp.arange(s, dtype=jnp.int32)
    row = (pos // GRID_W).astype(F32)
    colp = (pos % GRID_W).astype(F32)
    inv_freq = ROPE_THETA ** (-jnp.arange(ROT_FREQS, dtype=F32) / ROT_FREQS)
    ar = row[:, None] * inv_freq
    ac = colp[:, None] * inv_freq
    cos = jnp.concatenate([jnp.cos(ar), jnp.cos(ar), jnp.cos(ac), jnp.cos(ac)], axis=1)
    sin = jnp.concatenate([-jnp.sin(ar), jnp.sin(ar), -jnp.sin(ac), jnp.sin(ac)], axis=1)
    cos = jnp.concatenate([jnp.tile(cos, (nb, 1)), jnp.ones((nb * c, HD), F32)], axis=0)
    sin = jnp.concatenate([jnp.tile(sin, (nb, 1)), jnp.zeros((nb * c, HD), F32)], axis=0)
    return cos, sin


def kernel(x, c, ctx, c_ctx, pre_norm_mix, post_norm_mix, pre_norm_ffn, post_norm_ffn, w_mod, b_mod, fnet_w_out, gqa_w_qkv, gqa_q_norm, gqa_k_norm, gqa_w_o, swa_w_qkv, swa_sink, swa_w_o, router_group_w, router_group_b, router_expert_w, router_expert_b, expert_w_gate, expert_w_up, expert_w_down):
    nb, s, _ = x.shape
    cl = ctx.shape[1]
    depth = w_mod.shape[0]
    r_lat = nb * s
    r_all = r_lat + nb * cl
    assert s % TQ_FLASH == 0 and s % (N2 * SUB) == 0 and (nb * cl) % TM_QKV == 0 and r_all % (s // N2) == 0

    x_all = jnp.concatenate([x.reshape(r_lat, D), ctx.reshape(nb * cl, D)], axis=0)
    cvec = jnp.zeros((SUB, D), F32).at[:nb].set(c).at[nb].set(c_ctx)
    mods = _modulation(cvec, w_mod, b_mod)[:, :nb + 1].reshape(depth, nb + 1, NMOD, D)
    cos_e, sin_e = _rope_tables(nb, s, cl)

    for i in range(depth):
        kind = i % 3
        j = i // 3
        ctx_out = i < depth - 1
        mod = mods[i]
        g_pre = pre_norm_mix[i].reshape(1, D)
        g_post = post_norm_mix[i].reshape(1, D)

        if kind == 0:
            mixed, mixed_ctx = _fourier_mix(x_all, g_pre, mod, nb, s, cl, ctx_out)
            w_out = fnet_w_out[j].astype(BF16)
        else:
            if kind == 1:
                w_qkv, w_out = gqa_w_qkv[j], gqa_w_o[j].astype(BF16)
                qg, kg = gqa_q_norm[j], gqa_k_norm[j]
            else:
                w_qkv, w_out = swa_w_qkv[j], swa_w_o[j].astype(BF16)
                qg = kg = jnp.ones((HD,), F32)
            head_gain = jnp.concatenate([jnp.tile(qg * (ATTN_SCALE * LOG2E), NH), jnp.tile(kg, NKV), jnp.ones((NKV * HD,), F32)])
            head_gain = head_gain.reshape(1, QKV)
            qkv = _qkv_proj(x_all, g_pre, mod, w_qkv.astype(BF16), head_gain, cos_e, sin_e,
                            s // TM_QKV, nb, do_norm=(kind == 1))
            if kind == 1:
                mixed = _global_attention(qkv, nb, s, cl)
                sink = jnp.zeros((NH,), F32)
            else:
                sink = swa_sink[j]
                mixed = _window_attention(qkv, sink, nb, s, cl)
            mixed_ctx = _ctx_attention(qkv, sink, nb, s, cl, use_sink=(kind == 2)) if ctx_out else None

        rows = r_all if ctx_out else r_lat
        x_all = _oproj(mixed, mixed_ctx if ctx_out else mixed, w_out, x_all, g_post, mod,
                       s // TM_OPROJ, nb, ctx_out)

        wr = jnp.zeros((D, LANES), F32).at[:, :NG].set(router_group_w[i]).at[:, LANE_E0:LANE_E0 + NE].set(router_expert_w[i])
        br = jnp.zeros((1, LANES), F32).at[0, :NG].set(router_group_b[i]).at[0, LANE_E0:LANE_E0 + NE].set(router_expert_b[i])
        x_all = _moe(x_all, i, pre_norm_ffn[i].reshape(1, D), post_norm_ffn[i].reshape(1, D), mod, wr, br,
                     expert_w_gate, expert_w_up, expert_w_down, s // TM_MOE, nb, rows,
                     r_all if ctx_out else r_lat)
    return x_all.reshape(nb, s, D)
```

```python
import functools
import math

import numpy as np
import jax
import jax.numpy as jnp
from jax import lax
from jax.experimental import pallas as pl
from jax.experimental.pallas import tpu as pltpu

F32 = jnp.float32
BF16 = jnp.bfloat16
U32 = jnp.uint32

D = 2048
DP = D // 2
HD = 128
NH = 16
NKV = 4
GQ = NH // NKV
QKV = (NH + 2 * NKV) * HD
GRID_W = 64
ROPE_THETA = 10000.0
ROT_FREQS = HD // 4
WINDOW = 128
FG = 8
FC = D // FG
NG = 4
EPG = 8
NE = NG * EPG
FF = 512
NMOD = 6
EPS = 1e-6
ATTN_SCALE = HD ** -0.5
DEPTH = 4

LANES = 128
SUB = 8
VMEM_LIMIT = 56 * 1024 * 1024
VMEM_LIMIT_FLASH = 62 * 1024 * 1024

TM_QKV = 512
TN_QKV = 256
TM_OPROJ = 512
TQ_FLASH = 2048
TK_FLASH = 1024
FLASH_MARGIN = 60.0
LOG2E = math.log2(math.e)
TQ_WIN = 512
TM_MOE = 512
TMX = 512
TD_F = 512
N2 = 128
NEG = -1e30


def _cp(sem, vmem=VMEM_LIMIT):
    return pltpu.CompilerParams(dimension_semantics=sem, vmem_limit_bytes=vmem)


def _dot(a, b):
    return jnp.dot(a, b, preferred_element_type=F32)


def _dot_nt(a, b):
    return lax.dot_general(a, b, (((1,), (1,)), ((), ())), preferred_element_type=F32)


def _rms(x):
    return x * lax.rsqrt(jnp.mean(x * x, axis=-1, keepdims=True) + EPS)


def _mod_kernel(c_ref, w_ref, b_ref, o_ref):
    a = c_ref[...]
    a = a * jax.nn.sigmoid(a)
    o_ref[...] = _dot(a.astype(BF16), w_ref[...].astype(BF16)) + b_ref[...]


def _modulation(cvec, w_mod, b_mod):
    depth, _, nm = w_mod.shape
    tn = 1024
    return pl.pallas_call(
        _mod_kernel,
        grid=(depth, nm // tn),
        in_specs=[
            pl.BlockSpec((SUB, D), lambda l, j: (0, 0)),
            pl.BlockSpec((None, D, tn), lambda l, j: (l, 0, j)),
            pl.BlockSpec((None, 1, tn), lambda l, j: (l, 0, j)),
        ],
        out_specs=pl.BlockSpec((None, SUB, tn), lambda l, j: (l, 0, j)),
        out_shape=jax.ShapeDtypeStruct((depth, SUB, nm), F32),
        compiler_params=_cp(("arbitrary", "arbitrary")),
        name="modulation",
    )(cvec, w_mod, b_mod.reshape(depth, 1, nm))


def _qkv_kernel(x_ref, g_ref, mod_ref, w_ref, hg_ref, cos_ref, sin_ref, o_ref, *, do_norm, n_rot):
    h = _rms(x_ref[...]) * g_ref[...]
    h = (h * (1.0 + mod_ref[1:2, :]) + mod_ref[0:1, :]).astype(BF16)
    cos = cos_ref[...]
    sin = sin_ref[...]
    lane = lax.broadcasted_iota(jnp.int32, cos.shape, 1)
    first = (lane % (2 * ROT_FREQS)) < ROT_FREQS
    for j in range(QKV // TN_QKV):
        cols = slice(j * TN_QKV, (j + 1) * TN_QKV)
        y = _dot(h, w_ref[:, cols])
        if j >= n_rot:
            o_ref[:, cols] = y.astype(BF16)
            continue
        outs = []
        for hh in range(TN_QKV // HD):
            yh = y[:, hh * HD:(hh + 1) * HD]
            if do_norm:
                yh = _rms(yh)
            yh = yh * hg_ref[:, j * TN_QKV + hh * HD:j * TN_QKV + (hh + 1) * HD]
            partner = jnp.where(first, pltpu.roll(yh, HD - ROT_FREQS, 1), pltpu.roll(yh, ROT_FREQS, 1))
            outs.append(yh * cos + partner * sin)
        o_ref[:, cols] = jnp.concatenate(outs, axis=1).astype(BF16)


def _qkv_proj(x_all, gain, mod, w_bf, head_gain, cos_e, sin_e, tpb, nb, do_norm):
    r = x_all.shape[0]
    n_rot = (NH + NKV) * HD // TN_QKV
    return pl.pallas_call(
        functools.partial(_qkv_kernel, do_norm=do_norm, n_rot=n_rot),
        grid=(r // TM_QKV,),
        in_specs=[
            pl.BlockSpec((TM_QKV, D), lambda i: (i, 0)),
            pl.BlockSpec((1, D), lambda i: (0, 0)),
            pl.BlockSpec((None, NMOD, D), lambda i: (jnp.minimum(i // tpb, nb), 0, 0)),
            pl.BlockSpec((D, QKV), lambda i: (0, 0)),
            pl.BlockSpec((1, QKV), lambda i: (0, 0)),
            pl.BlockSpec((TM_QKV, HD), lambda i: (i, 0)),
            pl.BlockSpec((TM_QKV, HD), lambda i: (i, 0)),
        ],
        out_specs=pl.BlockSpec((TM_QKV, QKV), lambda i: (i, 0)),
        out_shape=jax.ShapeDtypeStruct((r, QKV), BF16),
        compiler_params=_cp(("parallel",)),
        name="qkv_proj",
    )(x_all, gain, mod, w_bf, head_gain, cos_e, sin_e)


def _flash_kernel(q_ref, kc_ref, vc_ref, k_ref, v_ref, o_ref, m_scr, acc_scr):
    j = pl.program_id(3)

    def chunks(s):
        return [s[:, c * LANES:(c + 1) * LANES] for c in range(s.shape[1] // LANES)]

    def rebase_update(k, v1):
        for g in range(GQ):
            ch = chunks(_dot_nt(q_ref[:, g * HD:(g + 1) * HD], k))
            m_prev = m_scr[g]
            m_new = jnp.maximum(m_prev, jnp.max(functools.reduce(jnp.maximum, ch), axis=-1, keepdims=True))
            alpha = jnp.exp2(m_prev - m_new)
            p = jnp.concatenate([jnp.exp2(c - m_new).astype(BF16) for c in ch], axis=1)
            acc = acc_scr[g]
            acc_scr[g] = jnp.concatenate([acc[:, :HD] * alpha, acc[:, HD:] * alpha], axis=1) + _dot(p, v1)
            m_scr[g] = m_new

    @pl.when(j == 0)
    def _():
        m_scr[...] = jnp.full(m_scr.shape, NEG, F32)
        acc_scr[...] = jnp.zeros(acc_scr.shape, F32)
        vc = vc_ref[...]
        rebase_update(kc_ref[...], jnp.concatenate([vc, jnp.ones_like(vc)], axis=1))

    k = k_ref[...]
    v = v_ref[...]
    v1 = jnp.concatenate([v, jnp.ones_like(v)], axis=1)
    ps = []
    excess = None
    for g in range(GQ):
        ch = chunks(_dot_nt(q_ref[:, g * HD:(g + 1) * HD], k))
        m = m_scr[g]
        over = functools.reduce(jnp.maximum, ch) - m
        excess = over if excess is None else jnp.maximum(excess, over)
        ps.append(jnp.concatenate([jnp.exp2(c - m).astype(BF16) for c in ch], axis=1))
    worst = jnp.max(jnp.max(excess, axis=0, keepdims=True), axis=1, keepdims=True)[0, 0]

    @pl.when(worst <= FLASH_MARGIN)
    def _():
        for g in range(GQ):
            acc_scr[g] += _dot(ps[g], v1)

    @pl.when(worst > FLASH_MARGIN)
    def _():
        rebase_update(k, v1)

    @pl.when(j == pl.num_programs(3) - 1)
    def _():
        for g in range(GQ):
            acc = acc_scr[g]
            o_ref[:, g * HD:(g + 1) * HD] = (acc[:, :HD] / acc[:, HD:]).astype(BF16)


def _global_attention(qkv, nb, s, c):
    nq = s // TQ_FLASH
    nk = s // TK_FLASH
    kcol = NH
    vcol = NH + NKV
    cblk = nb * s // c
    return pl.pallas_call(
        _flash_kernel,
        grid=(nb, NKV, nq, nk),
        in_specs=[
            pl.BlockSpec((TQ_FLASH, GQ * HD), lambda b, h, i, j: (b * nq + i, h), pipeline_mode=pl.Buffered(1)),
            pl.BlockSpec((c, HD), lambda b, h, i, j: (cblk + b, kcol + h)),
            pl.BlockSpec((c, HD), lambda b, h, i, j: (cblk + b, vcol + h)),
            pl.BlockSpec((TK_FLASH, HD), lambda b, h, i, j: (b * nk + j, kcol + h)),
            pl.BlockSpec((TK_FLASH, HD), lambda b, h, i, j: (b * nk + j, vcol + h)),
        ],
        out_specs=pl.BlockSpec((TQ_FLASH, GQ * HD), lambda b, h, i, j: (b * nq + i, h), pipeline_mode=pl.Buffered(1)),
        out_shape=jax.ShapeDtypeStruct((nb * s, NH * HD), BF16),
        scratch_shapes=[
            pltpu.VMEM((GQ, TQ_FLASH, LANES), F32),
            pltpu.VMEM((GQ, TQ_FLASH, 2 * HD), F32),
        ],
        compiler_params=_cp(("parallel", "parallel", "parallel", "arbitrary"), vmem=VMEM_LIMIT_FLASH),
        name="global_attention",
    )(qkv, qkv, qkv, qkv, qkv)


def _window_kernel(sink_ref, q_ref, kc_ref, vc_ref, kp_ref, vp_ref, k0_ref, v0_ref, kn_ref, vn_ref, bias_ref, o_ref):
    h = pl.program_id(1)
    k = jnp.concatenate([kc_ref[...], kp_ref[...], k0_ref[...], kn_ref[...]], axis=0)
    v = jnp.concatenate([vc_ref[...], vp_ref[...], v0_ref[...], vn_ref[...]], axis=0)
    v1 = jnp.concatenate([v, jnp.ones_like(v)], axis=1)
    bias = bias_ref[...]
    for g in range(GQ):
        sink = sink_ref[h * GQ + g] * LOG2E
        s = _dot_nt(q_ref[:, g * HD:(g + 1) * HD], k) + bias
        m = jnp.maximum(jnp.max(s, axis=-1, keepdims=True), sink)
        acc = _dot(jnp.exp2(s - m).astype(BF16), v1)
        o_ref[:, g * HD:(g + 1) * HD] = (acc[:, :HD] / (acc[:, HD:] + jnp.exp2(sink - m))).astype(BF16)


def _window_bias(c):
    q = jnp.arange(TQ_WIN, dtype=jnp.int32)[:, None]
    kk = jnp.arange(WINDOW, dtype=jnp.int32)[None, :]
    before = kk >= q
    after = q >= TQ_WIN - WINDOW + kk
    inside = jnp.abs(q - jnp.arange(TQ_WIN, dtype=jnp.int32)[None, :]) <= WINDOW
    ctx = jnp.ones((TQ_WIN, c), bool)
    variants = []
    for var in range(4):
        ok = jnp.concatenate([ctx, before & ((var & 1) == 0), inside, after & ((var & 2) == 0)], axis=1)
        variants.append(jnp.where(ok, 0.0, NEG).astype(F32))
    return jnp.stack(variants)


def _window_attention(qkv, sink, nb, s, c):
    tq = TQ_WIN
    nq = s // tq
    per = tq // WINDOW
    nw = s // WINDOW
    kcol = NH
    vcol = NH + NKV
    cblk = nb * s // c

    def edge(col, off):
        return pl.BlockSpec((WINDOW, HD), lambda b, h, i: (b * nw + jnp.clip(i * per + off, 0, nw - 1), col + h))

    def tile(col):
        return pl.BlockSpec((tq, HD), lambda b, h, i: (b * nq + i, col + h))

    return pl.pallas_call(
        _window_kernel,
        grid=(nb, NKV, nq),
        in_specs=[
            pl.BlockSpec(memory_space=pltpu.SMEM),
            pl.BlockSpec((tq, GQ * HD), lambda b, h, i: (b * nq + i, h)),
            pl.BlockSpec((c, HD), lambda b, h, i: (cblk + b, kcol + h)),
            pl.BlockSpec((c, HD), lambda b, h, i: (cblk + b, vcol + h)),
            edge(kcol, -1), edge(vcol, -1), tile(kcol), tile(vcol), edge(kcol, per), edge(vcol, per),
            pl.BlockSpec((None, tq, c + 2 * WINDOW + tq),
                         lambda b, h, i: ((i == 0).astype(jnp.int32) + 2 * (i == nq - 1).astype(jnp.int32), 0, 0)),
        ],
        out_specs=pl.BlockSpec((tq, GQ * HD), lambda b, h, i: (b * nq + i, h)),
        out_shape=jax.ShapeDtypeStruct((nb * s, NH * HD), BF16),
        compiler_params=_cp(("parallel", "parallel", "parallel")),
        name="window_attention",
    )(sink, qkv, qkv, qkv, qkv, qkv, qkv, qkv, qkv, qkv, _window_bias(c))


def _ctx_attn_kernel(sink_ref, q_ref, k_ref, v_ref, o_ref, *, use_sink):
    h = pl.program_id(1)
    k = k_ref[...]
    v = v_ref[...]
    for g in range(GQ):
        s = _dot_nt(q_ref[:, g * HD:(g + 1) * HD], k)
        m = jnp.max(s, axis=-1, keepdims=True)
        if use_sink:
            sink = sink_ref[h * GQ + g] * LOG2E
            m = jnp.maximum(m, sink)
        p = jnp.exp2(s - m)
        l = jnp.sum(p, axis=-1, keepdims=True)
        if use_sink:
            l = l + jnp.exp2(sink - m)
        o_ref[:, g * HD:(g + 1) * HD] = (_dot(p.astype(BF16), v) / l).astype(BF16)


def _ctx_attention(qkv, sink, nb, s, c, use_sink):
    cblk = nb * s // c
    return pl.pallas_call(
        functools.partial(_ctx_attn_kernel, use_sink=use_sink),
        grid=(nb, NKV),
        in_specs=[
            pl.BlockSpec(memory_space=pltpu.SMEM),
            pl.BlockSpec((c, GQ * HD), lambda b, h: (cblk + b, h)),
            pl.BlockSpec((c, HD), lambda b, h: (cblk + b, NH + h)),
            pl.BlockSpec((c, HD), lambda b, h: (cblk + b, NH + NKV + h)),
        ],
        out_specs=pl.BlockSpec((c, GQ * HD), lambda b, h: (b, h)),
        out_shape=jax.ShapeDtypeStruct((nb * c, NH * HD), BF16),
        compiler_params=_cp(("arbitrary", "arbitrary")),
        name="ctx_attention",
    )(sink, qkv, qkv, qkv)


def _oproj_kernel(a_ref, ac_ref, w_ref, x_ref, g_ref, mod_ref, o_ref, *, nlat):
    def project(src_ref):
        y = _dot(src_ref[...].astype(BF16), w_ref[...])
        o_ref[...] = x_ref[...] + mod_ref[2:3, :] * (_rms(y) * g_ref[...])

    i = pl.program_id(0)

    @pl.when(i < nlat)
    def _():
        project(a_ref)

    @pl.when(i >= nlat)
    def _():
        project(ac_ref)


def _oproj(a_lat, a_ctx, w_bf, x_all, gain, mod, tpb, nb, with_ctx):
    tm = TM_OPROJ
    nlat = a_lat.shape[0] // tm
    ntiles = nlat + (a_ctx.shape[0] // tm if with_ctx else 0)
    return pl.pallas_call(
        functools.partial(_oproj_kernel, nlat=nlat),
        grid=(ntiles,),
        in_specs=[
            pl.BlockSpec((tm, D), lambda i: (jnp.minimum(i, nlat - 1), 0)),
            pl.BlockSpec((tm, D), lambda i: (jnp.maximum(i - nlat, 0), 0)),
            pl.BlockSpec((D, D), lambda i: (0, 0)),
            pl.BlockSpec((tm, D), lambda i: (i, 0)),
            pl.BlockSpec((1, D), lambda i: (0, 0)),
            pl.BlockSpec((None, NMOD, D), lambda i: (jnp.minimum(i // tpb, nb), 0, 0)),
        ],
        out_specs=pl.BlockSpec((tm, D), lambda i: (i, 0)),
        out_shape=jax.ShapeDtypeStruct(x_all.shape, F32),
        input_output_aliases={3: 0},
        compiler_params=_cp(("parallel",)),
        name="oproj",
    )(a_lat, a_ctx, w_bf, x_all, gain, mod)


def _dft_mats(n):
    k = np.arange(n)
    ang = 2.0 * np.pi * ((k[:, None] * k[None, :]) % n) / n
    return np.cos(ang) / math.sqrt(n), np.sin(ang) / math.sqrt(n)


def _stage1_matrix(n1):
    c, s = _dft_mats(n1)
    eye = np.eye(SUB)
    return np.concatenate([np.kron(c, eye), -np.kron(s, eye)], axis=0)


def _stage2_matrix():
    c, s = _dft_mats(N2)
    return np.block([[c, s], [-s, c]])


def _channel_matrix():
    c, s = _dft_mats(FC)
    return np.concatenate([c, s], axis=0)


def _fft1_kernel(x_ref, g_ref, mod_ref, kr_ref, tc_ref, ts_ref, ar_ref, ai_ref, rstd_scr):
    dt = pl.program_id(2)
    n1 = x_ref.shape[0]
    rows = n1 * SUB

    @pl.when(dt == 0)
    def _():
        x = x_ref[...].reshape(rows, D)
        rstd_scr[...] = lax.rsqrt(jnp.mean(x * x, axis=-1, keepdims=True) + EPS)

    col = pl.multiple_of(dt * TD_F, TD_F)
    x = x_ref[:, :, pl.ds(col, TD_F)].reshape(rows, TD_F)
    h = x * rstd_scr[...] * g_ref[...]
    h = h * (1.0 + mod_ref[1:2, :]) + mod_ref[0:1, :]
    y = _dot(kr_ref[...], h.astype(BF16))
    yr = y[:rows]
    yi = y[rows:]
    tc = tc_ref[...]
    ts = ts_ref[...]
    for cc in range(TD_F // LANES):
        sl = slice(cc * LANES, (cc + 1) * LANES)
        a = yr[:, sl]
        b = yi[:, sl]
        ar_ref[:, :, sl] = (a * tc + b * ts).reshape(n1, SUB, LANES)
        ai_ref[:, :, sl] = (b * tc - a * ts).reshape(n1, SUB, LANES)


def _fft2_kernel(ar_ref, ai_ref, w2_ref, cs_ref, o_ref):
    yr, yi = [], []
    for jj in range(SUB):
        rows = slice(jj * N2, (jj + 1) * N2)
        a = jnp.concatenate([ar_ref[rows, :], ai_ref[rows, :]], axis=0).astype(BF16)
        y = _dot(w2_ref[...], a)
        yr.append(y[:N2])
        yi.append(y[N2:])
    yr = jnp.concatenate(yr, axis=0).astype(BF16)
    yi = jnp.concatenate(yi, axis=0).astype(BF16)
    for gi in range(TD_F // FC):
        sl = slice(gi * FC, (gi + 1) * FC)
        res = _dot(jnp.concatenate([yr[:, sl], yi[:, sl]], axis=1), cs_ref[...])
        for jj in range(SUB):
            o_ref[:, jj, sl] = res[jj * N2:(jj + 1) * N2]


def _fnet_ctx_kernel(x_ref, g_ref, mod_ref, f_ref, cs_ref, o_ref):
    c = x_ref.shape[0]
    h = _rms(x_ref[...]) * g_ref[...]
    h = h * (1.0 + mod_ref[1:2, :]) + mod_ref[0:1, :]
    y = _dot(f_ref[...], h.astype(BF16))
    yr = y[:c].astype(BF16)
    yi = y[c:].astype(BF16)
    for gi in range(FG):
        sl = slice(gi * FC, (gi + 1) * FC)
        z = jnp.concatenate([yr[:, sl], yi[:, sl]], axis=1)
        o_ref[:, sl] = _dot(z, cs_ref[...])


def _fourier_mix(x_all, gain, mod, nb, s, c, with_ctx):
    r = x_all.shape[0]
    n1 = s // N2
    nch = N2 // SUB
    ndt = D // TD_F
    kr = jnp.asarray(_stage1_matrix(n1), BF16)
    k2 = jnp.asarray(_stage2_matrix(), BF16)
    cs = jnp.asarray(_channel_matrix(), BF16)
    rows = n1 * SUB
    k1 = jnp.arange(n1, dtype=jnp.int32)[None, :, None]
    n2 = (jnp.arange(nch, dtype=jnp.int32)[:, None, None] * SUB + jnp.arange(SUB, dtype=jnp.int32)[None, None, :])
    ang = (2.0 * np.pi / s) * ((k1 * n2) % s).astype(F32).reshape(nch, rows, 1)
    tc = jnp.broadcast_to(jnp.cos(ang), (nch, rows, LANES))
    ts = jnp.broadcast_to(jnp.sin(ang), (nch, rows, LANES))

    x4 = x_all.reshape(r // N2, N2, D)
    ar, ai = pl.pallas_call(
        _fft1_kernel,
        grid=(nb, nch, ndt),
        in_specs=[
            pl.BlockSpec((n1, SUB, D), lambda b, ch, dt: (b, ch, 0)),
            pl.BlockSpec((1, TD_F), lambda b, ch, dt: (0, dt)),
            pl.BlockSpec((None, NMOD, TD_F), lambda b, ch, dt: (b, 0, dt)),
            pl.BlockSpec((2 * rows, rows), lambda b, ch, dt: (0, 0)),
            pl.BlockSpec((None, rows, LANES), lambda b, ch, dt: (ch, 0, 0)),
            pl.BlockSpec((None, rows, LANES), lambda b, ch, dt: (ch, 0, 0)),
        ],
        out_specs=[
            pl.BlockSpec((n1, SUB, TD_F), lambda b, ch, dt: (b, ch, dt)),
            pl.BlockSpec((n1, SUB, TD_F), lambda b, ch, dt: (b, ch, dt)),
        ],
        out_shape=[jax.ShapeDtypeStruct((nb * n1, N2, D), F32)] * 2,
        scratch_shapes=[pltpu.VMEM((rows, 1), F32)],
        compiler_params=_cp(("parallel", "parallel", "arbitrary")),
        name="fnet_stage1",
    )(x4, gain, mod, kr, tc, ts)

    ar2 = ar.reshape(nb * s, D)
    ai2 = ai.reshape(nb * s, D)
    nkc = n1 // SUB
    mixed = pl.pallas_call(
        _fft2_kernel,
        grid=(nb, nkc, ndt),
        in_specs=[
            pl.BlockSpec((N2 * SUB, TD_F), lambda b, kc, dt: (b * nkc + kc, dt)),
            pl.BlockSpec((N2 * SUB, TD_F), lambda b, kc, dt: (b * nkc + kc, dt)),
            pl.BlockSpec((2 * N2, 2 * N2), lambda b, kc, dt: (0, 0)),
            pl.BlockSpec((2 * FC, FC), lambda b, kc, dt: (0, 0)),
        ],
        out_specs=pl.BlockSpec((N2, SUB, TD_F), lambda b, kc, dt: (b, kc, dt)),
        out_shape=jax.ShapeDtypeStruct((nb * s // n1, n1, D), F32),
        compiler_params=_cp(("parallel", "parallel", "parallel")),
        name="fnet_stage2",
    )(ar2, ai2, k2, cs)
    mixed = mixed.reshape(nb * s, D)

    mixed_ctx = None
    if with_ctx:
        fc, fs = _dft_mats(c)
        fmat = jnp.asarray(np.concatenate([fc, -fs], axis=0), BF16)
        cblk = nb * s // c
        mixed_ctx = pl.pallas_call(
            _fnet_ctx_kernel,
            grid=(nb,),
            in_specs=[
                pl.BlockSpec((c, D), lambda b: (cblk + b, 0)),
                pl.BlockSpec((1, D), lambda b: (0, 0)),
                pl.BlockSpec((None, NMOD, D), lambda b: (nb, 0, 0)),
                pl.BlockSpec((2 * c, c), lambda b: (0, 0)),
                pl.BlockSpec((2 * FC, FC), lambda b: (0, 0)),
            ],
            out_specs=pl.BlockSpec((c, D), lambda b: (b, 0)),
            out_shape=jax.ShapeDtypeStruct((nb * c, D), F32),
            compiler_params=_cp(("arbitrary",)),
            name="fnet_ctx",
        )(x_all, gain, mod, fmat, cs)
    return mixed, mixed_ctx


M_E1, M_E2, M_R1, M_R2, M_W1, M_W2, M_PG = range(7)
LANE_E0 = NG


def _ffn_tokens(x_ref, g_ref, mod_ref):
    tok = _rms(x_ref[...]) * g_ref[...]
    return tok * (1.0 + mod_ref[4:5, :]) + mod_ref[3:4, :]


def _route_kernel(x_ref, g_ref, mod_ref, wr_ref, br_ref, meta_ref, cnt_ref, base_scr):
    i = pl.program_id(0)
    tm = x_ref.shape[0]

    @pl.when(i == 0)
    def _():
        base_scr[...] = jnp.zeros(base_scr.shape, F32)

    tok = _ffn_tokens(x_ref, g_ref, mod_ref)
    wr = wr_ref[...]
    w_hi = wr.astype(BF16)
    w_lo = (wr - w_hi.astype(F32)).astype(BF16)
    t_hi = tok.astype(BF16)
    t_lo = (tok - t_hi.astype(F32)).astype(BF16)
    logits = _dot(t_hi, w_hi) + (_dot(t_hi, w_lo) + _dot(t_lo, w_hi)) + br_ref[...]

    lane = lax.broadcasted_iota(jnp.int32, logits.shape, 1)
    big = jnp.int32(LANES)
    gmask = lane < NG
    gl = jnp.where(gmask, logits, NEG)
    gmax = jnp.max(gl, axis=-1, keepdims=True)
    gsum = jnp.sum(jnp.where(gmask, jnp.exp(gl - gmax), 0.0), axis=-1, keepdims=True)
    gidx = jnp.min(jnp.where(gl == gmax, lane, big), axis=-1, keepdims=True)
    lo = LANE_E0 + gidx * EPG
    emask = (lane >= lo) & (lane < lo + EPG)
    el = jnp.where(emask, logits, NEG)
    emax = jnp.max(el, axis=-1, keepdims=True)
    idx1 = jnp.min(jnp.where(el == emax, lane, big), axis=-1, keepdims=True)
    el2 = jnp.where(lane == idx1, NEG, el)
    e2max = jnp.max(el2, axis=-1, keepdims=True)
    idx2 = jnp.min(jnp.where(el2 == e2max, lane, big), axis=-1, keepdims=True)
    t = jnp.exp(e2max - emax)
    w1 = 1.0 / (1.0 + t)
    w2 = t / (1.0 + t)

    oh1 = (lane == idx1).astype(F32)
    oh2 = (lane == idx2).astype(F32)
    rr = lax.broadcasted_iota(jnp.int32, (tm, tm), 0)
    cc = lax.broadcasted_iota(jnp.int32, (tm, tm), 1)
    tri = (cc < rr).astype(BF16)
    base = base_scr[0:1, :]
    cnt1 = jnp.sum(oh1, axis=0, keepdims=True)
    cnt2 = jnp.sum(oh2, axis=0, keepdims=True)
    pre1 = _dot(tri, oh1.astype(BF16)) + base
    pre2 = _dot(tri, oh2.astype(BF16)) + base + cnt1
    r1 = jnp.sum(oh1 * pre1, axis=-1, keepdims=True)
    r2 = jnp.sum(oh2 * pre2, axis=-1, keepdims=True)
    new_base = base + cnt1 + cnt2
    base_scr[0:1, :] = new_base
    cnt_ref[...] = jnp.broadcast_to(new_base, cnt_ref.shape)

    rec = jnp.zeros(logits.shape, F32)
    for k, val in ((M_E1, (idx1 - LANE_E0).astype(F32)), (M_E2, (idx2 - LANE_E0).astype(F32)),
                   (M_R1, r1), (M_R2, r2), (M_W1, w1), (M_W2, w2), (M_PG, 1.0 / gsum)):
        rec = jnp.where(lane == k, val, rec)
    meta_ref[...] = rec


def _pack_pairs(t):
    half = t.shape[1] // 2
    hi = lax.bitcast_convert_type(t[:, :half].astype(BF16).astype(F32), U32)
    lo = lax.bitcast_convert_type(t[:, half:].astype(BF16).astype(F32), U32)
    return hi | (lo >> 16)


def _unpack_pairs(w):
    hi = lax.bitcast_convert_type(w & jnp.uint32(0xFFFF0000), F32)
    lo = lax.bitcast_convert_type(w << 16, F32)
    return jnp.concatenate([hi, lo], axis=1)


def _row_copy_wait(hbm_ref, vmem_ref, sem):
    pltpu.make_async_copy(hbm_ref.at[pl.ds(0, vmem_ref.shape[0])], vmem_ref, sem).wait()


def _dispatch_kernel(pt_ref, d_ref, x_ref, g_ref, mod_ref, xs_hbm, buf, zbuf, sem, zsem):
    i = pl.program_id(0)
    nt = pl.num_programs(0)
    tm = x_ref.shape[0]
    slot = i % 2

    @pl.when(i == 0)
    def _():
        zbuf[...] = jnp.zeros(zbuf.shape, U32)

        def zero_tile(t, carry):
            @pl.when(pt_ref[t] != 0)
            def _():
                pltpu.make_async_copy(zbuf, xs_hbm.at[pl.ds(pl.multiple_of(t * TMX, TMX), TMX)], zsem).start()
            return carry

        def zero_wait(t, carry):
            @pl.when(pt_ref[t] != 0)
            def _():
                _row_copy_wait(xs_hbm, zbuf, zsem)
            return carry

        lax.fori_loop(0, pt_ref.shape[0], zero_tile, 0)
        lax.fori_loop(0, pt_ref.shape[0], zero_wait, 0)

    @pl.when(i >= 2)
    def _():
        _row_copy_wait(xs_hbm, buf.at[slot], sem.at[slot])
        _row_copy_wait(xs_hbm, buf.at[slot], sem.at[slot])

    buf[slot] = _pack_pairs(_ffn_tokens(x_ref, g_ref, mod_ref))

    def issue(rw, carry):
        src = buf.at[slot, pl.ds(rw, 1)]
        pltpu.make_async_copy(src, xs_hbm.at[pl.ds(d_ref[0, rw], 1)], sem.at[slot]).start()
        pltpu.make_async_copy(src, xs_hbm.at[pl.ds(d_ref[0, tm + rw], 1)], sem.at[slot]).start()
        return carry

    for rw in range(tm):
        issue(rw, 0)

    @pl.when(i == nt - 1)
    def _():
        _row_copy_wait(xs_hbm, buf.at[slot], sem.at[slot])
        _row_copy_wait(xs_hbm, buf.at[slot], sem.at[slot])

        @pl.when(nt >= 2)
        def _():
            _row_copy_wait(xs_hbm, buf.at[1 - slot], sem.at[1 - slot])
            _row_copy_wait(xs_hbm, buf.at[1 - slot], sem.at[1 - slot])


def _expert_kernel(te_ref, nv_ref, xs_ref, wg_ref, wu_ref, wd_ref, o_ref, wg_s, wu_s, wd_s):
    t = pl.program_id(0)

    @pl.when(t < nv_ref[0])
    def _():
        prev = te_ref[jnp.maximum(t - 1, 0)]

        @pl.when((t == 0) | (te_ref[t] != prev))
        def _():
            wg_s[...] = wg_ref[...].astype(BF16)
            wu_s[...] = wu_ref[...].astype(BF16)
            wd_s[...] = wd_ref[...].astype(BF16)

        x = _unpack_pairs(xs_ref[...]).astype(BF16)
        a = _dot(x, wg_s[...])
        u = _dot(x, wu_s[...])
        hmid = (a * jax.nn.sigmoid(a) * u).astype(BF16)
        o_ref[...] = _pack_pairs(_dot(hmid, wd_s[...]))

    @pl.when(t >= nv_ref[0])
    def _():
        o_ref[...] = jnp.zeros(o_ref.shape, U32)


def _combine_kernel(d_ref, dn_ref, meta_ref, ys_hbm, x_ref, g_ref, mod_ref, o_ref, buf, sem):
    i = pl.program_id(0)
    nt = pl.num_programs(0)
    tm = x_ref.shape[0]
    slot = i % 2

    def gather(idx_ref, sl):
        def issue(rw, carry):
            pltpu.make_async_copy(ys_hbm.at[pl.ds(idx_ref[0, rw], 1)], buf.at[sl, 0, pl.ds(rw, 1)], sem.at[sl]).start()
            pltpu.make_async_copy(ys_hbm.at[pl.ds(idx_ref[0, tm + rw], 1)], buf.at[sl, 1, pl.ds(rw, 1)], sem.at[sl]).start()
            return carry

        for rw in range(tm):
            issue(rw, 0)

    @pl.when(i == 0)
    def _():
        gather(d_ref, 0)

    @pl.when(i + 1 < nt)
    def _():
        gather(dn_ref, 1 - slot)

    _row_copy_wait(ys_hbm, buf.at[slot, 0], sem.at[slot])
    _row_copy_wait(ys_hbm, buf.at[slot, 1], sem.at[slot])
    meta = meta_ref[...]
    w1 = meta[:, M_W1:M_W1 + 1]
    w2 = meta[:, M_W2:M_W2 + 1]
    pg = meta[:, M_PG:M_PG + 1]
    y = w1 * _unpack_pairs(buf[slot, 0]) + w2 * _unpack_pairs(buf[slot, 1])
    o_ref[...] = x_ref[...] + mod_ref[5:6, :] * (pg * (_rms(y) * g_ref[...]))


def _moe(x_all, layer, pre_gain, post_gain, mod, wr, br, w_gate, w_up, w_down, tpb, nb, nrows, out_rows):
    tm = TM_MOE
    nt = nrows // tm
    mod_map = lambda i: (jnp.minimum(i // tpb, nb), 0, 0)
    meta, cnt = pl.pallas_call(
        _route_kernel,
        grid=(nt,),
        in_specs=[
            pl.BlockSpec((tm, D), lambda i: (i, 0)),
            pl.BlockSpec((1, D), lambda i: (0, 0)),
            pl.BlockSpec((None, NMOD, D), mod_map),
            pl.BlockSpec((D, LANES), lambda i: (0, 0)),
            pl.BlockSpec((1, LANES), lambda i: (0, 0)),
        ],
        out_specs=[
            pl.BlockSpec((tm, LANES), lambda i: (i, 0)),
            pl.BlockSpec((SUB, LANES), lambda i: (0, 0)),
        ],
        out_shape=[
            jax.ShapeDtypeStruct((nrows, LANES), F32),
            jax.ShapeDtypeStruct((SUB, LANES), F32),
        ],
        scratch_shapes=[pltpu.VMEM((SUB, LANES), F32)],
        compiler_params=_cp(("arbitrary",)),
        name="moe_route",
    )(x_all, pre_gain, mod, wr, br)

    ntile = nrows * 2 // TMX + NE
    nslot = ntile * TMX
    e = meta[:, M_E1:M_E2 + 1].astype(jnp.int32)
    rk = meta[:, M_R1:M_R2 + 1].astype(jnp.int32)
    counts = cnt[0, LANE_E0:LANE_E0 + NE].astype(jnp.int32)
    padded = (counts + TMX - 1) // TMX * TMX
    seg_end = jnp.cumsum(padded)
    seg_start = seg_end - padded
    onehot = e[:, :, None] == jnp.arange(NE, dtype=jnp.int32)[None, None, :]
    dest = jnp.sum(jnp.where(onehot, seg_start[None, None, :], 0), axis=-1) + rk
    dtile = dest.reshape(nt, tm, 2).transpose(0, 2, 1).reshape(nt, 1, 2 * tm)
    nvalid = (seg_end[-1] // TMX).astype(jnp.int32).reshape(1)
    tile_id = jnp.arange(ntile, dtype=jnp.int32)
    te_raw = jnp.sum((tile_id[:, None] * TMX >= seg_end[None, :]).astype(jnp.int32), axis=1)
    te = jnp.minimum(te_raw, te_raw[jnp.maximum(nvalid[0] - 1, 0)])
    last_of_seg = jnp.concatenate([te_raw[1:] != te_raw[:-1], jnp.ones((1,), bool)])
    partial = (last_of_seg | (tile_id >= nvalid[0])).astype(jnp.int32)

    xs = pl.pallas_call(
        _dispatch_kernel,
        grid_spec=pltpu.PrefetchScalarGridSpec(
            num_scalar_prefetch=1,
            grid=(nt,),
            in_specs=[
                pl.BlockSpec((None, 1, 2 * tm), lambda i, pt: (i, 0, 0), memory_space=pltpu.SMEM),
                pl.BlockSpec((tm, D), lambda i, pt: (i, 0)),
                pl.BlockSpec((1, D), lambda i, pt: (0, 0)),
                pl.BlockSpec((None, NMOD, D), lambda i, pt: mod_map(i)),
            ],
            out_specs=pl.BlockSpec(memory_space=pl.ANY),
            scratch_shapes=[
                pltpu.VMEM((2, tm, DP), U32),
                pltpu.VMEM((TMX, DP), U32),
                pltpu.SemaphoreType.DMA((2,)),
                pltpu.SemaphoreType.DMA(()),
            ],
        ),
        out_shape=jax.ShapeDtypeStruct((nslot, DP), U32),
        compiler_params=_cp(("arbitrary",)),
        name="moe_dispatch",
    )(partial, dtile, x_all, pre_gain, mod)

    ys = pl.pallas_call(
        _expert_kernel,
        grid_spec=pltpu.PrefetchScalarGridSpec(
            num_scalar_prefetch=2,
            grid=(ntile,),
            in_specs=[
                pl.BlockSpec((TMX, DP), lambda t, te_r, nv_r: (jnp.minimum(t, nv_r[0] - 1), 0)),
                pl.BlockSpec((None, None, D, FF), lambda t, te_r, nv_r: (layer, te_r[t], 0, 0)),
                pl.BlockSpec((None, None, D, FF), lambda t, te_r, nv_r: (layer, te_r[t], 0, 0)),
                pl.BlockSpec((None, None, FF, D), lambda t, te_r, nv_r: (layer, te_r[t], 0, 0)),
            ],
            out_specs=pl.BlockSpec((TMX, DP), lambda t, te_r, nv_r: (t, 0)),
            scratch_shapes=[
                pltpu.VMEM((D, FF), BF16),
                pltpu.VMEM((D, FF), BF16),
                pltpu.VMEM((FF, D), BF16),
            ],
        ),
        out_shape=jax.ShapeDtypeStruct((nslot, DP), U32),
        compiler_params=_cp(("arbitrary",)),
        name="moe_experts",
    )(te, nvalid, xs, w_gate, w_up, w_down)

    alias = {} if out_rows != x_all.shape[0] else {4: 0}
    return pl.pallas_call(
        _combine_kernel,
        grid=(nt,),
        in_specs=[
            pl.BlockSpec((None, 1, 2 * tm), lambda i: (i, 0, 0), memory_space=pltpu.SMEM),
            pl.BlockSpec((None, 1, 2 * tm), lambda i: (jnp.minimum(i + 1, nt - 1), 0, 0), memory_space=pltpu.SMEM),
            pl.BlockSpec((tm, LANES), lambda i: (i, 0)),
            pl.BlockSpec(memory_space=pl.ANY),
            pl.BlockSpec((tm, D), lambda i: (i, 0)),
            pl.BlockSpec((1, D), lambda i: (0, 0)),
            pl.BlockSpec((None, NMOD, D), mod_map),
        ],
        out_specs=pl.BlockSpec((tm, D), lambda i: (i, 0)),
        out_shape=jax.ShapeDtypeStruct((out_rows, D), F32),
        scratch_shapes=[pltpu.VMEM((2, 2, tm, DP), U32), pltpu.SemaphoreType.DMA((2,))],
        input_output_aliases=alias,
        compiler_params=_cp(("arbitrary",)),
        name="moe_combine",
    )(dtile, dtile, meta, ys, x_all, post_gain, mod)


def _rope_tables(nb, s, c):
    pos = jnp.arange(s, dtype=jnp.int32)
    row = (pos // GRID_W).astype(F32)
    colp = (pos % GRID_W).astype(F32)
    inv_freq = ROPE_THETA ** (-jnp.arange(ROT_FREQS, dtype=F32) / ROT_FREQS)
    ar = row[:, None] * inv_freq
    ac = colp[:, None] * inv_freq
    cos = jnp.concatenate([jnp.cos(ar), jnp.cos(ar), jnp.cos(ac), jnp.cos(ac)], axis=1)
    sin = jnp.concatenate([-jnp.sin(ar), jnp.sin(ar), -jnp.sin(ac), jnp.sin(ac)], axis=1)
    cos = jnp.concatenate([jnp.tile(cos, (nb, 1)), jnp.ones((nb * c, HD), F32)], axis=0)
    sin = jnp.concatenate([jnp.tile(sin, (nb, 1)), jnp.zeros((nb * c, HD), F32)], axis=0)
    return cos, sin


def kernel(x, c, ctx, c_ctx, pre_norm_mix, post_norm_mix, pre_norm_ffn, post_norm_ffn, w_mod, b_mod, fnet_w_out, gqa_w_qkv, gqa_q_norm, gqa_k_norm, gqa_w_o, swa_w_qkv, swa_sink, swa_w_o, router_group_w, router_group_b, router_expert_w, router_expert_b, expert_w_gate, expert_w_up, expert_w_down):
    nb, s, _ = x.shape
    cl = ctx.shape[1]
    depth = w_mod.shape[0]
    r_lat = nb * s
    r_all = r_lat + nb * cl
    assert s % TQ_FLASH == 0 and s % (N2 * SUB) == 0 and (nb * cl) % TM_QKV == 0 and r_all % (s // N2) == 0

    x_all = jnp.concatenate([x.reshape(r_lat, D), ctx.reshape(nb * cl, D)], axis=0)
    cvec = jnp.zeros((SUB, D), F32).at[:nb].set(c).at[nb].set(c_ctx)
    mods = _modulation(cvec, w_mod, b_mod)[:, :nb + 1].reshape(depth, nb + 1, NMOD, D)
    cos_e, sin_e = _rope_tables(nb, s, cl)

    for i in range(depth):
        kind = i % 3
        j = i // 3
        ctx_out = i < depth - 1
        mod = mods[i]
        g_pre = pre_norm_mix[i].reshape(1, D)
        g_post = post_norm_mix[i].reshape(1, D)

        if kind == 0:
            mixed, mixed_ctx = _fourier_mix(x_all, g_pre, mod, nb, s, cl, ctx_out)
            w_out = fnet_w_out[j].astype(BF16)
        else:
            if kind == 1:
                w_qkv, w_out = gqa_w_qkv[j], gqa_w_o[j].astype(BF16)
                qg, kg = gqa_q_norm[j], gqa_k_norm[j]
            else:
                w_qkv, w_out = swa_w_qkv[j], swa_w_o[j].astype(BF16)
                qg = kg = jnp.ones((HD,), F32)
            head_gain = jnp.concatenate([jnp.tile(qg * (ATTN_SCALE * LOG2E), NH), jnp.tile(kg, NKV), jnp.ones((NKV * HD,), F32)])
            head_gain = head_gain.reshape(1, QKV)
            qkv = _qkv_proj(x_all, g_pre, mod, w_qkv.astype(BF16), head_gain, cos_e, sin_e,
                            s // TM_QKV, nb, do_norm=(kind == 1))
            if kind == 1:
                mixed = _global_attention(qkv, nb, s, cl)
                sink = jnp.zeros((NH,), F32)
            else:
                sink = swa_sink[j]
                mixed = _window_attention(qkv, sink, nb, s, cl)
            mixed_ctx = _ctx_attention(qkv, sink, nb, s, cl, use_sink=(kind == 2)) if ctx_out else None

        rows = r_all if ctx_out else r_lat
        x_all = _oproj(mixed, mixed_ctx if ctx_out else mixed, w_out, x_all, g_post, mod,
                       s // TM_OPROJ, nb, ctx_out)

        wr = jnp.zeros((D, LANES), F32).at[:, :NG].set(router_group_w[i]).at[:, LANE_E0:LANE_E0 + NE].set(router_expert_w[i])
        br = jnp.zeros((1, LANES), F32).at[0, :NG].set(router_group_b[i]).at[0, LANE_E0:LANE_E0 + NE].set(router_expert_b[i])
        x_all = _moe(x_all, i, pre_norm_ffn[i].reshape(1, D), post_norm_ffn[i].reshape(1, D), mod, wr, br,
                     expert_w_gate, expert_w_up, expert_w_down, s // TM_MOE, nb, rows,
                     r_all if ctx_out else r_lat)
    return x_all.reshape(nb, s, D)
```

```python
import functools
import math

import numpy as np
import jax
import jax.numpy as jnp
from jax import lax
from jax.experimental import pallas as pl
from jax.experimental.pallas import tpu as pltpu

F32 = jnp.float32
BF16 = jnp.bfloat16
U32 = jnp.uint32

D = 2048
DP = D // 2
HD = 128
NH = 16
NKV = 4
GQ = NH // NKV
QKV = (NH + 2 * NKV) * HD
GRID_W = 64
ROPE_THETA = 10000.0
ROT_FREQS = HD // 4
WINDOW = 128
FG = 8
FC = D // FG
NG = 4
EPG = 8
NE = NG * EPG
FF = 512
NMOD = 6
EPS = 1e-6
ATTN_SCALE = HD ** -0.5
DEPTH = 4

LANES = 128
SUB = 8
VMEM_LIMIT = 56 * 1024 * 1024
VMEM_LIMIT_FLASH = 62 * 1024 * 1024

TM_QKV = 512
TN_QKV = 256
TM_OPROJ = 512
TQ_FLASH = 2048
TK_FLASH = 1024
FLASH_MARGIN = 60.0
LOG2E = math.log2(math.e)
TQ_WIN = 512
TM_MOE = 512
TM_COMBINE = 256
TMX = 512
TD_F = 512
N2 = 128
NEG = -1e30


def _cp(sem, vmem=VMEM_LIMIT):
    return pltpu.CompilerParams(dimension_semantics=sem, vmem_limit_bytes=vmem)


def _dot(a, b):
    return jnp.dot(a, b, preferred_element_type=F32)


def _dot_nt(a, b):
    return lax.dot_general(a, b, (((1,), (1,)), ((), ())), preferred_element_type=F32)


def _rms(x):
    return x * lax.rsqrt(jnp.mean(x * x, axis=-1, keepdims=True) + EPS)


def _mod_kernel(c_ref, w_ref, b_ref, o_ref):
    a = c_ref[...]
    a = a * jax.nn.sigmoid(a)
    o_ref[...] = _dot(a.astype(BF16), w_ref[...].astype(BF16)) + b_ref[...]


def _modulation(cvec, w_mod, b_mod):
    depth, _, nm = w_mod.shape
    tn = 1024
    return pl.pallas_call(
        _mod_kernel,
        grid=(depth, nm // tn),
        in_specs=[
            pl.BlockSpec((SUB, D), lambda l, j: (0, 0)),
            pl.BlockSpec((None, D, tn), lambda l, j: (l, 0, j)),
            pl.BlockSpec((None, 1, tn), lambda l, j: (l, 0, j)),
        ],
        out_specs=pl.BlockSpec((None, SUB, tn), lambda l, j: (l, 0, j)),
        out_shape=jax.ShapeDtypeStruct((depth, SUB, nm), F32),
        compiler_params=_cp(("arbitrary", "arbitrary")),
        name="modulation",
    )(cvec, w_mod, b_mod.reshape(depth, 1, nm))


def _qkv_kernel(x_ref, g_ref, mod_ref, w_ref, hg_ref, cos_ref, sin_ref, o_ref, *, do_norm, n_rot):
    h = _rms(x_ref[...]) * g_ref[...]
    h = (h * (1.0 + mod_ref[1:2, :]) + mod_ref[0:1, :]).astype(BF16)
    cos = cos_ref[...]
    sin = sin_ref[...]
    lane = lax.broadcasted_iota(jnp.int32, cos.shape, 1)
    first = (lane % (2 * ROT_FREQS)) < ROT_FREQS
    for j in range(QKV // TN_QKV):
        cols = slice(j * TN_QKV, (j + 1) * TN_QKV)
        y = _dot(h, w_ref[:, cols])
        if j >= n_rot:
            o_ref[:, cols] = y.astype(BF16)
            continue
        outs = []
        for hh in range(TN_QKV // HD):
            yh = y[:, hh * HD:(hh + 1) * HD]
            if do_norm:
                yh = _rms(yh)
            yh = yh * hg_ref[:, j * TN_QKV + hh * HD:j * TN_QKV + (hh + 1) * HD]
            partner = jnp.where(first, pltpu.roll(yh, HD - ROT_FREQS, 1), pltpu.roll(yh, ROT_FREQS, 1))
            outs.append(yh * cos + partner * sin)
        o_ref[:, cols] = jnp.concatenate(outs, axis=1).astype(BF16)


def _qkv_proj(x_all, gain, mod, w_bf, head_gain, cos_e, sin_e, tpb, nb, do_norm):
    r = x_all.shape[0]
    n_rot = (NH + NKV) * HD // TN_QKV
    return pl.pallas_call(
        functools.partial(_qkv_kernel, do_norm=do_norm, n_rot=n_rot),
        grid=(r // TM_QKV,),
        in_specs=[
            pl.BlockSpec((TM_QKV, D), lambda i: (i, 0)),
            pl.BlockSpec((1, D), lambda i: (0, 0)),
            pl.BlockSpec((None, NMOD, D), lambda i: (jnp.minimum(i // tpb, nb), 0, 0)),
            pl.BlockSpec((D, QKV), lambda i: (0, 0)),
            pl.BlockSpec((1, QKV), lambda i: (0, 0)),
            pl.BlockSpec((TM_QKV, HD), lambda i: (i, 0)),
            pl.BlockSpec((TM_QKV, HD), lambda i: (i, 0)),
        ],
        out_specs=pl.BlockSpec((TM_QKV, QKV), lambda i: (i, 0)),
        out_shape=jax.ShapeDtypeStruct((r, QKV), BF16),
        compiler_params=_cp(("parallel",)),
        name="qkv_proj",
    )(x_all, gain, mod, w_bf, head_gain, cos_e, sin_e)


def _flash_kernel(q_ref, kc_ref, vc_ref, k_ref, v_ref, o_ref, m_scr, acc_scr):
    j = pl.program_id(3)

    def chunks(s):
        return [s[:, c * LANES:(c + 1) * LANES] for c in range(s.shape[1] // LANES)]

    def rebase_update(k, v1):
        for g in range(GQ):
            ch = chunks(_dot_nt(q_ref[:, g * HD:(g + 1) * HD], k))
            m_prev = m_scr[g]
            m_new = jnp.maximum(m_prev, jnp.max(functools.reduce(jnp.maximum, ch), axis=-1, keepdims=True))
            alpha = jnp.exp2(m_prev - m_new)
            p = jnp.concatenate([jnp.exp2(c - m_new).astype(BF16) for c in ch], axis=1)
            acc = acc_scr[g]
            acc_scr[g] = jnp.concatenate([acc[:, :HD] * alpha, acc[:, HD:] * alpha], axis=1) + _dot(p, v1)
            m_scr[g] = m_new

    @pl.when(j == 0)
    def _():
        m_scr[...] = jnp.full(m_scr.shape, NEG, F32)
        acc_scr[...] = jnp.zeros(acc_scr.shape, F32)
        vc = vc_ref[...]
        rebase_update(kc_ref[...], jnp.concatenate([vc, jnp.ones_like(vc)], axis=1))

    k = k_ref[...]
    v = v_ref[...]
    v1 = jnp.concatenate([v, jnp.ones_like(v)], axis=1)
    ps = []
    excess = None
    for g in range(GQ):
        ch = chunks(_dot_nt(q_ref[:, g * HD:(g + 1) * HD], k))
        m = m_scr[g]
        over = functools.reduce(jnp.maximum, ch) - m
        excess = over if excess is None else jnp.maximum(excess, over)
        ps.append(jnp.concatenate([jnp.exp2(c - m).astype(BF16) for c in ch], axis=1))
    worst = jnp.max(jnp.max(excess, axis=0, keepdims=True), axis=1, keepdims=True)[0, 0]

    @pl.when(worst <= FLASH_MARGIN)
    def _():
        for g in range(GQ):
            acc_scr[g] += _dot(ps[g], v1)

    @pl.when(worst > FLASH_MARGIN)
    def _():
        rebase_update(k, v1)

    @pl.when(j == pl.num_programs(3) - 1)
    def _():
        for g in range(GQ):
            acc = acc_scr[g]
            o_ref[:, g * HD:(g + 1) * HD] = (acc[:, :HD] / acc[:, HD:]).astype(BF16)


def _global_attention(qkv, nb, s, c):
    nq = s // TQ_FLASH
    nk = s // TK_FLASH
    kcol = NH
    vcol = NH + NKV
    cblk = nb * s // c
    return pl.pallas_call(
        _flash_kernel,
        grid=(nb, NKV, nq, nk),
        in_specs=[
            pl.BlockSpec((TQ_FLASH, GQ * HD), lambda b, h, i, j: (b * nq + i, h), pipeline_mode=pl.Buffered(1)),
            pl.BlockSpec((c, HD), lambda b, h, i, j: (cblk + b, kcol + h)),
            pl.BlockSpec((c, HD), lambda b, h, i, j: (cblk + b, vcol + h)),
            pl.BlockSpec((TK_FLASH, HD), lambda b, h, i, j: (b * nk + j, kcol + h)),
            pl.BlockSpec((TK_FLASH, HD), lambda b, h, i, j: (b * nk + j, vcol + h)),
        ],
        out_specs=pl.BlockSpec((TQ_FLASH, GQ * HD), lambda b, h, i, j: (b * nq + i, h), pipeline_mode=pl.Buffered(1)),
        out_shape=jax.ShapeDtypeStruct((nb * s, NH * HD), BF16),
        scratch_shapes=[
            pltpu.VMEM((GQ, TQ_FLASH, LANES), F32),
            pltpu.VMEM((GQ, TQ_FLASH, 2 * HD), F32),
        ],
        compiler_params=_cp(("parallel", "parallel", "parallel", "arbitrary"), vmem=VMEM_LIMIT_FLASH),
        name="global_attention",
    )(qkv, qkv, qkv, qkv, qkv)


def _window_kernel(sink_ref, q_ref, kc_ref, vc_ref, kp_ref, vp_ref, k0_ref, v0_ref, kn_ref, vn_ref, bias_ref, o_ref):
    h = pl.program_id(1)
    k = jnp.concatenate([kc_ref[...], kp_ref[...], k0_ref[...], kn_ref[...]], axis=0)
    v = jnp.concatenate([vc_ref[...], vp_ref[...], v0_ref[...], vn_ref[...]], axis=0)
    v1 = jnp.concatenate([v, jnp.ones_like(v)], axis=1)
    bias = bias_ref[...]
    for g in range(GQ):
        sink = sink_ref[h * GQ + g] * LOG2E
        s = _dot_nt(q_ref[:, g * HD:(g + 1) * HD], k) + bias
        m = jnp.maximum(jnp.max(s, axis=-1, keepdims=True), sink)
        acc = _dot(jnp.exp2(s - m).astype(BF16), v1)
        o_ref[:, g * HD:(g + 1) * HD] = (acc[:, :HD] / (acc[:, HD:] + jnp.exp2(sink - m))).astype(BF16)


def _window_bias(c):
    q = jnp.arange(TQ_WIN, dtype=jnp.int32)[:, None]
    kk = jnp.arange(WINDOW, dtype=jnp.int32)[None, :]
    before = kk >= q
    after = q >= TQ_WIN - WINDOW + kk
    inside = jnp.abs(q - jnp.arange(TQ_WIN, dtype=jnp.int32)[None, :]) <= WINDOW
    ctx = jnp.ones((TQ_WIN, c), bool)
    variants = []
    for var in range(4):
        ok = jnp.concatenate([ctx, before & ((var & 1) == 0), inside, after & ((var & 2) == 0)], axis=1)
        variants.append(jnp.where(ok, 0.0, NEG).astype(F32))
    return jnp.stack(variants)


def _window_attention(qkv, sink, nb, s, c):
    tq = TQ_WIN
    nq = s // tq
    per = tq // WINDOW
    nw = s // WINDOW
    kcol = NH
    vcol = NH + NKV
    cblk = nb * s // c

    def edge(col, off):
        return pl.BlockSpec((WINDOW, HD), lambda b, h, i: (b * nw + jnp.clip(i * per + off, 0, nw - 1), col + h))

    def tile(col):
        return pl.BlockSpec((tq, HD), lambda b, h, i: (b * nq + i, col + h))

    return pl.pallas_call(
        _window_kernel,
        grid=(nb, NKV, nq),
        in_specs=[
            pl.BlockSpec(memory_space=pltpu.SMEM),
            pl.BlockSpec((tq, GQ * HD), lambda b, h, i: (b * nq + i, h)),
            pl.BlockSpec((c, HD), lambda b, h, i: (cblk + b, kcol + h)),
            pl.BlockSpec((c, HD), lambda b, h, i: (cblk + b, vcol + h)),
            edge(kcol, -1), edge(vcol, -1), tile(kcol), tile(vcol), edge(kcol, per), edge(vcol, per),
            pl.BlockSpec((None, tq, c + 2 * WINDOW + tq),
                         lambda b, h, i: ((i == 0).astype(jnp.int32) + 2 * (i == nq - 1).astype(jnp.int32), 0, 0)),
        ],
        out_specs=pl.BlockSpec((tq, GQ * HD), lambda b, h, i: (b * nq + i, h)),
        out_shape=jax.ShapeDtypeStruct((nb * s, NH * HD), BF16),
        compiler_params=_cp(("parallel", "parallel", "parallel")),
        name="window_attention",
    )(sink, qkv, qkv, qkv, qkv, qkv, qkv, qkv, qkv, qkv, _window_bias(c))


def _ctx_attn_kernel(sink_ref, q_ref, k_ref, v_ref, o_ref, *, use_sink):
    h = pl.program_id(1)
    k = k_ref[...]
    v = v_ref[...]
    for g in range(GQ):
        s = _dot_nt(q_ref[:, g * HD:(g + 1) * HD], k)
        m = jnp.max(s, axis=-1, keepdims=True)
        if use_sink:
            sink = sink_ref[h * GQ + g] * LOG2E
            m = jnp.maximum(m, sink)
        p = jnp.exp2(s - m)
        l = jnp.sum(p, axis=-1, keepdims=True)
        if use_sink:
            l = l + jnp.exp2(sink - m)
        o_ref[:, g * HD:(g + 1) * HD] = (_dot(p.astype(BF16), v) / l).astype(BF16)


def _ctx_attention(qkv, sink, nb, s, c, use_sink):
    cblk = nb * s // c
    return pl.pallas_call(
        functools.partial(_ctx_attn_kernel, use_sink=use_sink),
        grid=(nb, NKV),
        in_specs=[
            pl.BlockSpec(memory_space=pltpu.SMEM),
            pl.BlockSpec((c, GQ * HD), lambda b, h: (cblk + b, h)),
            pl.BlockSpec((c, HD), lambda b, h: (cblk + b, NH + h)),
            pl.BlockSpec((c, HD), lambda b, h: (cblk + b, NH + NKV + h)),
        ],
        out_specs=pl.BlockSpec((c, GQ * HD), lambda b, h: (b, h)),
        out_shape=jax.ShapeDtypeStruct((nb * c, NH * HD), BF16),
        compiler_params=_cp(("arbitrary", "arbitrary")),
        name="ctx_attention",
    )(sink, qkv, qkv, qkv)


def _oproj_kernel(a_ref, ac_ref, w_ref, x_ref, g_ref, mod_ref, o_ref, *, nlat):
    def project(src_ref):
        y = _dot(src_ref[...].astype(BF16), w_ref[...])
        o_ref[...] = x_ref[...] + mod_ref[2:3, :] * (_rms(y) * g_ref[...])

    i = pl.program_id(0)

    @pl.when(i < nlat)
    def _():
        project(a_ref)

    @pl.when(i >= nlat)
    def _():
        project(ac_ref)


def _oproj(a_lat, a_ctx, w_bf, x_all, gain, mod, tpb, nb, with_ctx):
    tm = TM_OPROJ
    nlat = a_lat.shape[0] // tm
    ntiles = nlat + (a_ctx.shape[0] // tm if with_ctx else 0)
    return pl.pallas_call(
        functools.partial(_oproj_kernel, nlat=nlat),
        grid=(ntiles,),
        in_specs=[
            pl.BlockSpec((tm, D), lambda i: (jnp.minimum(i, nlat - 1), 0)),
            pl.BlockSpec((tm, D), lambda i: (jnp.maximum(i - nlat, 0), 0)),
            pl.BlockSpec((D, D), lambda i: (0, 0)),
            pl.BlockSpec((tm, D), lambda i: (i, 0)),
            pl.BlockSpec((1, D), lambda i: (0, 0)),
            pl.BlockSpec((None, NMOD, D), lambda i: (jnp.minimum(i // tpb, nb), 0, 0)),
        ],
        out_specs=pl.BlockSpec((tm, D), lambda i: (i, 0)),
        out_shape=jax.ShapeDtypeStruct(x_all.shape, F32),
        input_output_aliases={3: 0},
        compiler_params=_cp(("parallel",)),
        name="oproj",
    )(a_lat, a_ctx, w_bf, x_all, gain, mod)


def _dft_mats(n):
    k = np.arange(n)
    ang = 2.0 * np.pi * ((k[:, None] * k[None, :]) % n) / n
    return np.cos(ang) / math.sqrt(n), np.sin(ang) / math.sqrt(n)


def _stage1_matrix(n1):
    c, s = _dft_mats(n1)
    eye = np.eye(SUB)
    return np.concatenate([np.kron(c, eye), -np.kron(s, eye)], axis=0)


def _stage2_matrix():
    c, s = _dft_mats(N2)
    return np.block([[c, s], [-s, c]])


def _channel_matrix():
    c, s = _dft_mats(FC)
    return np.concatenate([c, s], axis=0)


def _fft1_kernel(x_ref, g_ref, mod_ref, kr_ref, tc_ref, ts_ref, ar_ref, ai_ref, rstd_scr):
    dt = pl.program_id(2)
    n1 = x_ref.shape[0]
    rows = n1 * SUB

    @pl.when(dt == 0)
    def _():
        x = x_ref[...].reshape(rows, D)
        rstd_scr[...] = lax.rsqrt(jnp.mean(x * x, axis=-1, keepdims=True) + EPS)

    col = pl.multiple_of(dt * TD_F, TD_F)
    x = x_ref[:, :, pl.ds(col, TD_F)].reshape(rows, TD_F)
    h = x * rstd_scr[...] * g_ref[...]
    h = h * (1.0 + mod_ref[1:2, :]) + mod_ref[0:1, :]
    y = _dot(kr_ref[...], h.astype(BF16))
    yr = y[:rows]
    yi = y[rows:]
    tc = tc_ref[...]
    ts = ts_ref[...]
    for cc in range(TD_F // LANES):
        sl = slice(cc * LANES, (cc + 1) * LANES)
        a = yr[:, sl]
        b = yi[:, sl]
        ar_ref[:, :, sl] = (a * tc + b * ts).reshape(n1, SUB, LANES)
        ai_ref[:, :, sl] = (b * tc - a * ts).reshape(n1, SUB, LANES)


def _fft2_kernel(ar_ref, ai_ref, w2_ref, cs_ref, o_ref):
    yr, yi = [], []
    for jj in range(SUB):
        rows = slice(jj * N2, (jj + 1) * N2)
        a = jnp.concatenate([ar_ref[rows, :], ai_ref[rows, :]], axis=0).astype(BF16)
        y = _dot(w2_ref[...], a)
        yr.append(y[:N2])
        yi.append(y[N2:])
    yr = jnp.concatenate(yr, axis=0).astype(BF16)
    yi = jnp.concatenate(yi, axis=0).astype(BF16)
    for gi in range(TD_F // FC):
        sl = slice(gi * FC, (gi + 1) * FC)
        res = _dot(jnp.concatenate([yr[:, sl], yi[:, sl]], axis=1), cs_ref[...])
        for jj in range(SUB):
            o_ref[:, jj, sl] = res[jj * N2:(jj + 1) * N2]


def _fnet_ctx_kernel(x_ref, g_ref, mod_ref, f_ref, cs_ref, o_ref):
    c = x_ref.shape[0]
    h = _rms(x_ref[...]) * g_ref[...]
    h = h * (1.0 + mod_ref[1:2, :]) + mod_ref[0:1, :]
    y = _dot(f_ref[...], h.astype(BF16))
    yr = y[:c].astype(BF16)
    yi = y[c:].astype(BF16)
    for gi in range(FG):
        sl = slice(gi * FC, (gi + 1) * FC)
        z = jnp.concatenate([yr[:, sl], yi[:, sl]], axis=1)
        o_ref[:, sl] = _dot(z, cs_ref[...])


def _fourier_mix(x_all, gain, mod, nb, s, c, with_ctx):
    r = x_all.shape[0]
    n1 = s // N2
    nch = N2 // SUB
    ndt = D // TD_F
    kr = jnp.asarray(_stage1_matrix(n1), BF16)
    k2 = jnp.asarray(_stage2_matrix(), BF16)
    cs = jnp.asarray(_channel_matrix(), BF16)
    rows = n1 * SUB
    k1 = jnp.arange(n1, dtype=jnp.int32)[None, :, None]
    n2 = (jnp.arange(nch, dtype=jnp.int32)[:, None, None] * SUB + jnp.arange(SUB, dtype=jnp.int32)[None, None, :])
    ang = (2.0 * np.pi / s) * ((k1 * n2) % s).astype(F32).reshape(nch, rows, 1)
    tc = jnp.broadcast_to(jnp.cos(ang), (nch, rows, LANES))
    ts = jnp.broadcast_to(jnp.sin(ang), (nch, rows, LANES))

    x4 = x_all.reshape(r // N2, N2, D)
    ar, ai = pl.pallas_call(
        _fft1_kernel,
        grid=(nb, nch, ndt),
        in_specs=[
            pl.BlockSpec((n1, SUB, D), lambda b, ch, dt: (b, ch, 0)),
            pl.BlockSpec((1, TD_F), lambda b, ch, dt: (0, dt)),
            pl.BlockSpec((None, NMOD, TD_F), lambda b, ch, dt: (b, 0, dt)),
            pl.BlockSpec((2 * rows, rows), lambda b, ch, dt: (0, 0)),
            pl.BlockSpec((None, rows, LANES), lambda b, ch, dt: (ch, 0, 0)),
            pl.BlockSpec((None, rows, LANES), lambda b, ch, dt: (ch, 0, 0)),
        ],
        out_specs=[
            pl.BlockSpec((n1, SUB, TD_F), lambda b, ch, dt: (b, ch, dt)),
            pl.BlockSpec((n1, SUB, TD_F), lambda b, ch, dt: (b, ch, dt)),
        ],
        out_shape=[jax.ShapeDtypeStruct((nb * n1, N2, D), F32)] * 2,
        scratch_shapes=[pltpu.VMEM((rows, 1), F32)],
        compiler_params=_cp(("parallel", "parallel", "arbitrary")),
        name="fnet_stage1",
    )(x4, gain, mod, kr, tc, ts)

    ar2 = ar.reshape(nb * s, D)
    ai2 = ai.reshape(nb * s, D)
    nkc = n1 // SUB
    mixed = pl.pallas_call(
        _fft2_kernel,
        grid=(nb, nkc, ndt),
        in_specs=[
            pl.BlockSpec((N2 * SUB, TD_F), lambda b, kc, dt: (b * nkc + kc, dt)),
            pl.BlockSpec((N2 * SUB, TD_F), lambda b, kc, dt: (b * nkc + kc, dt)),
            pl.BlockSpec((2 * N2, 2 * N2), lambda b, kc, dt: (0, 0)),
            pl.BlockSpec((2 * FC, FC), lambda b, kc, dt: (0, 0)),
        ],
        out_specs=pl.BlockSpec((N2, SUB, TD_F), lambda b, kc, dt: (b, kc, dt)),
        out_shape=jax.ShapeDtypeStruct((nb * s // n1, n1, D), F32),
        compiler_params=_cp(("parallel", "parallel", "parallel")),
        name="fnet_stage2",
    )(ar2, ai2, k2, cs)
    mixed = mixed.reshape(nb * s, D)

    mixed_ctx = None
    if with_ctx:
        fc, fs = _dft_mats(c)
        fmat = jnp.asarray(np.concatenate([fc, -fs], axis=0), BF16)
        cblk = nb * s // c
        mixed_ctx = pl.pallas_call(
            _fnet_ctx_kernel,
            grid=(nb,),
            in_specs=[
                pl.BlockSpec((c, D), lambda b: (cblk + b, 0)),
                pl.BlockSpec((1, D), lambda b: (0, 0)),
                pl.BlockSpec((None, NMOD, D), lambda b: (nb, 0, 0)),
                pl.BlockSpec((2 * c, c), lambda b: (0, 0)),
                pl.BlockSpec((2 * FC, FC), lambda b: (0, 0)),
            ],
            out_specs=pl.BlockSpec((c, D), lambda b: (b, 0)),
            out_shape=jax.ShapeDtypeStruct((nb * c, D), F32),
            compiler_params=_cp(("arbitrary",)),
            name="fnet_ctx",
        )(x_all, gain, mod, fmat, cs)
    return mixed, mixed_ctx


M_E1, M_E2, M_R1, M_R2, M_W1, M_W2, M_PG = range(7)
LANE_E0 = NG


def _ffn_tokens(x_ref, g_ref, mod_ref):
    tok = _rms(x_ref[...]) * g_ref[...]
    return tok * (1.0 + mod_ref[4:5, :]) + mod_ref[3:4, :]


def _route_kernel(x_ref, g_ref, mod_ref, wr_ref, br_ref, meta_ref, cnt_ref, base_scr):
    i = pl.program_id(0)
    tm = x_ref.shape[0]

    @pl.when(i == 0)
    def _():
        base_scr[...] = jnp.zeros(base_scr.shape, F32)

    tok = _ffn_tokens(x_ref, g_ref, mod_ref)
    wr = wr_ref[...]
    w_hi = wr.astype(BF16)
    w_lo = (wr - w_hi.astype(F32)).astype(BF16)
    t_hi = tok.astype(BF16)
    t_lo = (tok - t_hi.astype(F32)).astype(BF16)
    logits = _dot(t_hi, w_hi) + (_dot(t_hi, w_lo) + _dot(t_lo, w_hi)) + br_ref[...]

    lane = lax.broadcasted_iota(jnp.int32, logits.shape, 1)
    big = jnp.int32(LANES)
    gmask = lane < NG
    gl = jnp.where(gmask, logits, NEG)
    gmax = jnp.max(gl, axis=-1, keepdims=True)
    gsum = jnp.sum(jnp.where(gmask, jnp.exp(gl - gmax), 0.0), axis=-1, keepdims=True)
    gidx = jnp.min(jnp.where(gl == gmax, lane, big), axis=-1, keepdims=True)
    lo = LANE_E0 + gidx * EPG
    emask = (lane >= lo) & (lane < lo + EPG)
    el = jnp.where(emask, logits, NEG)
    emax = jnp.max(el, axis=-1, keepdims=True)
    idx1 = jnp.min(jnp.where(el == emax, lane, big), axis=-1, keepdims=True)
    el2 = jnp.where(lane == idx1, NEG, el)
    e2max = jnp.max(el2, axis=-1, keepdims=True)
    idx2 = jnp.min(jnp.where(el2 == e2max, lane, big), axis=-1, keepdims=True)
    t = jnp.exp(e2max - emax)
    w1 = 1.0 / (1.0 + t)
    w2 = t / (1.0 + t)

    oh1 = (lane == idx1).astype(F32)
    oh2 = (lane == idx2).astype(F32)
    rr = lax.broadcasted_iota(jnp.int32, (tm, tm), 0)
    cc = lax.broadcasted_iota(jnp.int32, (tm, tm), 1)
    tri = (cc < rr).astype(BF16)
    base = base_scr[0:1, :]
    cnt1 = jnp.sum(oh1, axis=0, keepdims=True)
    cnt2 = jnp.sum(oh2, axis=0, keepdims=True)
    pre1 = _dot(tri, oh1.astype(BF16)) + base
    pre2 = _dot(tri, oh2.astype(BF16)) + base + cnt1
    r1 = jnp.sum(oh1 * pre1, axis=-1, keepdims=True)
    r2 = jnp.sum(oh2 * pre2, axis=-1, keepdims=True)
    new_base = base + cnt1 + cnt2
    base_scr[0:1, :] = new_base
    cnt_ref[...] = jnp.broadcast_to(new_base, cnt_ref.shape)

    rec = jnp.zeros(logits.shape, F32)
    for k, val in ((M_E1, (idx1 - LANE_E0).astype(F32)), (M_E2, (idx2 - LANE_E0).astype(F32)),
                   (M_R1, r1), (M_R2, r2), (M_W1, w1), (M_W2, w2), (M_PG, 1.0 / gsum)):
        rec = jnp.where(lane == k, val, rec)
    meta_ref[...] = rec


def _pack_pairs(t):
    half = t.shape[1] // 2
    hi = lax.bitcast_convert_type(t[:, :half].astype(BF16).astype(F32), U32)
    lo = lax.bitcast_convert_type(t[:, half:].astype(BF16).astype(F32), U32)
    return hi | (lo >> 16)


def _unpack_pairs(w):
    hi = lax.bitcast_convert_type(w & jnp.uint32(0xFFFF0000), F32)
    lo = lax.bitcast_convert_type(w << 16, F32)
    return jnp.concatenate([hi, lo], axis=1)


def _row_copy_wait(hbm_ref, vmem_ref, sem):
    pltpu.make_async_copy(hbm_ref.at[pl.ds(0, vmem_ref.shape[0])], vmem_ref, sem).wait()


def _dispatch_kernel(pt_ref, d_ref, x_ref, g_ref, mod_ref, xs_hbm, buf, zbuf, sem, zsem):
    i = pl.program_id(0)
    nt = pl.num_programs(0)
    tm = x_ref.shape[0]
    slot = i % 2

    @pl.when(i == 0)
    def _():
        zbuf[...] = jnp.zeros(zbuf.shape, U32)

        def zero_tile(t, carry):
            @pl.when(pt_ref[t] != 0)
            def _():
                pltpu.make_async_copy(zbuf, xs_hbm.at[pl.ds(pl.multiple_of(t * TMX, TMX), TMX)], zsem).start()
            return carry

        def zero_wait(t, carry):
            @pl.when(pt_ref[t] != 0)
            def _():
                _row_copy_wait(xs_hbm, zbuf, zsem)
            return carry

        lax.fori_loop(0, pt_ref.shape[0], zero_tile, 0)
        lax.fori_loop(0, pt_ref.shape[0], zero_wait, 0)

    @pl.when(i >= 2)
    def _():
        _row_copy_wait(xs_hbm, buf.at[slot], sem.at[slot])
        _row_copy_wait(xs_hbm, buf.at[slot], sem.at[slot])

    buf[slot] = _pack_pairs(_ffn_tokens(x_ref, g_ref, mod_ref))

    def issue(rw, carry):
        src = buf.at[slot, pl.ds(rw, 1)]
        pltpu.make_async_copy(src, xs_hbm.at[pl.ds(d_ref[0, rw], 1)], sem.at[slot]).start()
        pltpu.make_async_copy(src, xs_hbm.at[pl.ds(d_ref[0, tm + rw], 1)], sem.at[slot]).start()
        return carry

    for rw in range(tm):
        issue(rw, 0)

    @pl.when(i == nt - 1)
    def _():
        _row_copy_wait(xs_hbm, buf.at[slot], sem.at[slot])
        _row_copy_wait(xs_hbm, buf.at[slot], sem.at[slot])

        @pl.when(nt >= 2)
        def _():
            _row_copy_wait(xs_hbm, buf.at[1 - slot], sem.at[1 - slot])
            _row_copy_wait(xs_hbm, buf.at[1 - slot], sem.at[1 - slot])


def _expert_kernel(te_ref, nv_ref, xs_ref, wg_ref, wu_ref, wd_ref, o_ref, wg_s, wu_s, wd_s):
    t = pl.program_id(0)

    @pl.when(t < nv_ref[0])
    def _():
        prev = te_ref[jnp.maximum(t - 1, 0)]

        @pl.when((t == 0) | (te_ref[t] != prev))
        def _():
            wg_s[...] = wg_ref[...].astype(BF16)
            wu_s[...] = wu_ref[...].astype(BF16)
            wd_s[...] = wd_ref[...].astype(BF16)

        x = _unpack_pairs(xs_ref[...]).astype(BF16)
        a = _dot(x, wg_s[...])
        u = _dot(x, wu_s[...])
        hmid = (a * jax.nn.sigmoid(a) * u).astype(BF16)
        o_ref[...] = _pack_pairs(_dot(hmid, wd_s[...]))

    @pl.when(t >= nv_ref[0])
    def _():
        o_ref[...] = jnp.zeros(o_ref.shape, U32)


def _combine_kernel(d_ref, dn_ref, meta_ref, ys_hbm, x_ref, g_ref, mod_ref, o_ref, buf, sem):
    i = pl.program_id(0)
    nt = pl.num_programs(0)
    tm = x_ref.shape[0]
    slot = i % 2

    def gather(idx_ref, sl):
        def issue(rw, carry):
            pltpu.make_async_copy(ys_hbm.at[pl.ds(idx_ref[0, rw], 1)], buf.at[sl, 0, pl.ds(rw, 1)], sem.at[sl]).start()
            pltpu.make_async_copy(ys_hbm.at[pl.ds(idx_ref[0, tm + rw], 1)], buf.at[sl, 1, pl.ds(rw, 1)], sem.at[sl]).start()
            return carry

        for rw in range(tm):
            issue(rw, 0)

    @pl.when(i == 0)
    def _():
        gather(d_ref, 0)

    @pl.when(i + 1 < nt)
    def _():
        gather(dn_ref, 1 - slot)

    _row_copy_wait(ys_hbm, buf.at[slot, 0], sem.at[slot])
    _row_copy_wait(ys_hbm, buf.at[slot, 1], sem.at[slot])
    meta = meta_ref[...]
    w1 = meta[:, M_W1:M_W1 + 1]
    w2 = meta[:, M_W2:M_W2 + 1]
    pg = meta[:, M_PG:M_PG + 1]
    y = w1 * _unpack_pairs(buf[slot, 0]) + w2 * _unpack_pairs(buf[slot, 1])
    o_ref[...] = x_ref[...] + mod_ref[5:6, :] * (pg * (_rms(y) * g_ref[...]))


def _moe(x_all, layer, pre_gain, post_gain, mod, wr, br, w_gate, w_up, w_down, tpb, nb, nrows, out_rows):
    tm = TM_MOE
    nt = nrows // tm
    mod_map = lambda i: (jnp.minimum(i // tpb, nb), 0, 0)
    meta, cnt = pl.pallas_call(
        _route_kernel,
        grid=(nt,),
        in_specs=[
            pl.BlockSpec((tm, D), lambda i: (i, 0)),
            pl.BlockSpec((1, D), lambda i: (0, 0)),
            pl.BlockSpec((None, NMOD, D), mod_map),
            pl.BlockSpec((D, LANES), lambda i: (0, 0)),
            pl.BlockSpec((1, LANES), lambda i: (0, 0)),
        ],
        out_specs=[
            pl.BlockSpec((tm, LANES), lambda i: (i, 0)),
            pl.BlockSpec((SUB, LANES), lambda i: (0, 0)),
        ],
        out_shape=[
            jax.ShapeDtypeStruct((nrows, LANES), F32),
            jax.ShapeDtypeStruct((SUB, LANES), F32),
        ],
        scratch_shapes=[pltpu.VMEM((SUB, LANES), F32)],
        compiler_params=_cp(("arbitrary",)),
        name="moe_route",
    )(x_all, pre_gain, mod, wr, br)

    ntile = nrows * 2 // TMX + NE
    nslot = ntile * TMX
    e = meta[:, M_E1:M_E2 + 1].astype(jnp.int32)
    rk = meta[:, M_R1:M_R2 + 1].astype(jnp.int32)
    counts = cnt[0, LANE_E0:LANE_E0 + NE].astype(jnp.int32)
    padded = (counts + TMX - 1) // TMX * TMX
    seg_end = jnp.cumsum(padded)
    seg_start = seg_end - padded
    onehot = e[:, :, None] == jnp.arange(NE, dtype=jnp.int32)[None, None, :]
    dest = jnp.sum(jnp.where(onehot, seg_start[None, None, :], 0), axis=-1) + rk
    dtile = dest.reshape(nt, tm, 2).transpose(0, 2, 1).reshape(nt, 1, 2 * tm)
    nvalid = (seg_end[-1] // TMX).astype(jnp.int32).reshape(1)
    tile_id = jnp.arange(ntile, dtype=jnp.int32)
    te_raw = jnp.sum((tile_id[:, None] * TMX >= seg_end[None, :]).astype(jnp.int32), axis=1)
    te = jnp.minimum(te_raw, te_raw[jnp.maximum(nvalid[0] - 1, 0)])
    last_of_seg = jnp.concatenate([te_raw[1:] != te_raw[:-1], jnp.ones((1,), bool)])
    partial = (last_of_seg | (tile_id >= nvalid[0])).astype(jnp.int32)

    xs = pl.pallas_call(
        _dispatch_kernel,
        grid_spec=pltpu.PrefetchScalarGridSpec(
            num_scalar_prefetch=1,
            grid=(nt,),
            in_specs=[
                pl.BlockSpec((None, 1, 2 * tm), lambda i, pt: (i, 0, 0), memory_space=pltpu.SMEM),
                pl.BlockSpec((tm, D), lambda i, pt: (i, 0)),
                pl.BlockSpec((1, D), lambda i, pt: (0, 0)),
                pl.BlockSpec((None, NMOD, D), lambda i, pt: mod_map(i)),
            ],
            out_specs=pl.BlockSpec(memory_space=pl.ANY),
            scratch_shapes=[
                pltpu.VMEM((2, tm, DP), U32),
                pltpu.VMEM((TMX, DP), U32),
                pltpu.SemaphoreType.DMA((2,)),
                pltpu.SemaphoreType.DMA(()),
            ],
        ),
        out_shape=jax.ShapeDtypeStruct((nslot, DP), U32),
        compiler_params=_cp(("arbitrary",)),
        name="moe_dispatch",
    )(partial, dtile, x_all, pre_gain, mod)

    ys = pl.pallas_call(
        _expert_kernel,
        grid_spec=pltpu.PrefetchScalarGridSpec(
            num_scalar_prefetch=2,
            grid=(ntile,),
            in_specs=[
                pl.BlockSpec((TMX, DP), lambda t, te_r, nv_r: (jnp.minimum(t, nv_r[0] - 1), 0)),
                pl.BlockSpec((None, None, D, FF), lambda t, te_r, nv_r: (layer, te_r[t], 0, 0)),
                pl.BlockSpec((None, None, D, FF), lambda t, te_r, nv_r: (layer, te_r[t], 0, 0)),
                pl.BlockSpec((None, None, FF, D), lambda t, te_r, nv_r: (layer, te_r[t], 0, 0)),
            ],
            out_specs=pl.BlockSpec((TMX, DP), lambda t, te_r, nv_r: (t, 0)),
            scratch_shapes=[
                pltpu.VMEM((D, FF), BF16),
                pltpu.VMEM((D, FF), BF16),
                pltpu.VMEM((FF, D), BF16),
            ],
        ),
        out_shape=jax.ShapeDtypeStruct((nslot, DP), U32),
        compiler_params=_cp(("arbitrary",)),
        name="moe_experts",
    )(te, nvalid, xs, w_gate, w_up, w_down)

    tc = TM_COMBINE
    ntc = nrows // tc
    ctile = dest.reshape(ntc, tc, 2).transpose(0, 2, 1).reshape(ntc, 1, 2 * tc)
    cmod_map = lambda i: (jnp.minimum(i // (tpb * tm // tc), nb), 0, 0)
    alias = {} if out_rows != x_all.shape[0] else {4: 0}
    return pl.pallas_call(
        _combine_kernel,
        grid=(ntc,),
        in_specs=[
            pl.BlockSpec((None, 1, 2 * tc), lambda i: (i, 0, 0), memory_space=pltpu.SMEM),
            pl.BlockSpec((None, 1, 2 * tc), lambda i: (jnp.minimum(i + 1, ntc - 1), 0, 0), memory_space=pltpu.SMEM),
            pl.BlockSpec((tc, LANES), lambda i: (i, 0)),
            pl.BlockSpec(memory_space=pl.ANY),
            pl.BlockSpec((tc, D), lambda i: (i, 0)),
            pl.BlockSpec((1, D), lambda i: (0, 0)),
            pl.BlockSpec((None, NMOD, D), cmod_map),
        ],
        out_specs=pl.BlockSpec((tc, D), lambda i: (i, 0)),
        out_shape=jax.ShapeDtypeStruct((out_rows, D), F32),
        scratch_shapes=[pltpu.VMEM((2, 2, tc, DP), U32), pltpu.SemaphoreType.DMA((2,))],
        input_output_aliases=alias,
        compiler_params=_cp(("arbitrary",)),
        name="moe_combine",
    )(ctile, ctile, meta, ys, x_all, post_gain, mod)


def _rope_tables(nb, s, c):
    pos = jnp.arange(s, dtype=jnp.int32)
    row = (pos // GRID_W).astype(F32)
    colp = (pos % GRID_W).astype(F32)
    inv_freq = ROPE_THETA ** (-jnp.arange(ROT_FREQS, dtype=F32) / ROT_FREQS)
    ar = row[:, None] * inv_freq
    ac = colp[:, None] * inv_freq
    cos = jnp.concatenate([jnp.cos(ar), jnp.cos(ar), jnp.cos(ac), jnp.cos(ac)], axis=1)
    sin = jnp.concatenate([-jnp.sin(ar), jnp.sin(ar), -jnp.sin(ac), jnp.sin(ac)], axis=1)
    cos = jnp.concatenate([jnp.tile(cos, (nb, 1)), jnp.ones((nb * c, HD), F32)], axis=0)
    sin = jnp.concatenate([jnp.tile(sin, (nb, 1)), jnp.zeros((nb * c, HD), F32)], axis=0)
    return cos, sin


def kernel(x, c, ctx, c_ctx, pre_norm_mix, post_norm_mix, pre_norm_ffn, post_norm_ffn, w_mod, b_mod, fnet_w_out, gqa_w_qkv, gqa_q_norm, gqa_k_norm, gqa_w_o, swa_w_qkv, swa_sink, swa_w_o, router_group_w, router_group_b, router_expert_w, router_expert_b, expert_w_gate, expert_w_up, expert_w_down):
    nb, s, _ = x.shape
    cl = ctx.shape[1]
    depth = w_mod.shape[0]
    r_lat = nb * s
    r_all = r_lat + nb * cl
    assert s % TQ_FLASH == 0 and s % (N2 * SUB) == 0 and (nb * cl) % TM_QKV == 0 and r_all % (s // N2) == 0

    x_all = jnp.concatenate([x.reshape(r_lat, D), ctx.reshape(nb * cl, D)], axis=0)
    cvec = jnp.zeros((SUB, D), F32).at[:nb].set(c).at[nb].set(c_ctx)
    mods = _modulation(cvec, w_mod, b_mod)[:, :nb + 1].reshape(depth, nb + 1, NMOD, D)
    cos_e, sin_e = _rope_tables(nb, s, cl)

    for i in range(depth):
        kind = i % 3
        j = i // 3
        ctx_out = i < depth - 1
        mod = mods[i]
        g_pre = pre_norm_mix[i].reshape(1, D)
        g_post = post_norm_mix[i].reshape(1, D)

        if kind == 0:
            mixed, mixed_ctx = _fourier_mix(x_all, g_pre, mod, nb, s, cl, ctx_out)
            w_out = fnet_w_out[j].astype(BF16)
        else:
            if kind == 1:
                w_qkv, w_out = gqa_w_qkv[j], gqa_w_o[j].astype(BF16)
                qg, kg = gqa_q_norm[j], gqa_k_norm[j]
            else:
                w_qkv, w_out = swa_w_qkv[j], swa_w_o[j].astype(BF16)
                qg = kg = jnp.ones((HD,), F32)
            head_gain = jnp.concatenate([jnp.tile(qg * (ATTN_SCALE * LOG2E), NH), jnp.tile(kg, NKV), jnp.ones((NKV * HD,), F32)])
            head_gain = head_gain.reshape(1, QKV)
            qkv = _qkv_proj(x_all, g_pre, mod, w_qkv.astype(BF16), head_gain, cos_e, sin_e,
                            s // TM_QKV, nb, do_norm=(kind == 1))
            if kind == 1:
                mixed = _global_attention(qkv, nb, s, cl)
                sink = jnp.zeros((NH,), F32)
            else:
                sink = swa_sink[j]
                mixed = _window_attention(qkv, sink, nb, s, cl)
            mixed_ctx = _ctx_attention(qkv, sink, nb, s, cl, use_sink=(kind == 2)) if ctx_out else None

        rows = r_all if ctx_out else r_lat
        x_all = _oproj(mixed, mixed_ctx if ctx_out else mixed, w_out, x_all, g_post, mod,
                       s // TM_OPROJ, nb, ctx_out)

        wr = jnp.zeros((D, LANES), F32).at[:, :NG].set(router_group_w[i]).at[:, LANE_E0:LANE_E0 + NE].set(router_expert_w[i])
        br = jnp.zeros((1, LANES), F32).at[0, :NG].set(router_group_b[i]).at[0, LANE_E0:LANE_E0 + NE].set(router_expert_b[i])
        x_all = _moe(x_all, i, pre_norm_ffn[i].reshape(1, D), post_norm_ffn[i].reshape(1, D), mod, wr, br,
                     expert_w_gate, expert_w_up, expert_w_down, s // TM_MOE, nb, rows,
                     r_all if ctx_out else r_lat)
    return x_all.reshape(nb, s, D)
```
